```python
import math
import jax
import jax.numpy as jnp
from jax import lax
import numpy as np

D_MODEL = 1024
BATCH = 32
SEQ = 256
DEPTH = 2
DEC_BATCH = 4
DEC_SEQ = 4096
PAST_LEN = 256

GRID_W = 64
N_MIXERS = 2
N_CONV_LAYERS = (DEPTH + 1) // 2
N_DELTA_LAYERS = DEPTH // 2
CONV_K = 3
HK = 8
HV = 16
DK = 128
DV = 128
QK_DIM = HK * DK
V_DIM = HV * DV
CONV_CH = 2 * QK_DIM + V_DIM
DN_IN = CONV_CH + V_DIM + 4 * HV
CHUNK = 64
N_EXPERTS = 32
TOP_K = 4
D_FF = 1024
SWIGLU_ALPHA = 1.702
SWIGLU_LIMIT = 7.0
EXPERT_BLOCK = 128
EPS = 1e-6

kernel_name = "hybrid_conv_deltanet_moe_diffusion_step"


def _rmsnorm(x, w):
    xf = x.astype(jnp.float32)
    y = xf * lax.rsqrt(jnp.mean(xf * xf, axis=-1, keepdims=True) + EPS)
    return (y * w.astype(jnp.float32)).astype(x.dtype)


def _l2norm(x):
    x = x.astype(jnp.float32)
    return x * lax.rsqrt(jnp.sum(x * x, axis=-1, keepdims=True) + EPS)


def _short_conv(u, w, row_len):
    bsz, t, ch = u.shape
    rows = t // row_len
    k = w.shape[0]
    p = k // 2
    ur = jnp.pad(u.reshape(bsz, rows, row_len, ch), ((0, 0), (0, 0), (p, p), (0, 0)))
    out = sum(ur[:, :, j:j + row_len] * w[j] for j in range(k))
    return out.reshape(bsz, t, ch)


def _conv_mixer(h, w_in, w_conv, w_out, row_len):
    b, cg, u = jnp.split(h @ w_in, 3, axis=-1)
    return (b * _short_conv(cg * u, w_conv, row_len)) @ w_out


def _chunk_gated_delta(q, k, v, beta, g, s0):
    bsz, t, h, _ = q.shape
    n = t // CHUNK

    def blk(a):
        return jnp.moveaxis(a.reshape((bsz, n, CHUNK, h) + a.shape[3:]), 3, 1)

    q, k, v, beta, g = (blk(a) for a in (q, k, v, beta, g))
    gc = jnp.cumsum(g, axis=-1)
    tril = jnp.tril(jnp.ones((CHUNK, CHUNK), dtype=bool))
    eye = jnp.eye(CHUNK, dtype=bool)
    decay = jnp.exp(jnp.where(tril, gc[..., :, None] - gc[..., None, :], -jnp.inf))
    kb = k * beta[..., None]
    a = jnp.einsum('bhncd,bhnsd->bhncs', kb, k) * decay
    lower = jnp.where(eye, 1.0, a)
    rhs = jnp.concatenate([v * beta[..., None], kb * jnp.exp(gc)[..., None]], axis=-1)
    sol = lax.linalg.triangular_solve(lower, rhs, left_side=True, lower=True, unit_diagonal=True)
    u, w = sol[..., :DV], sol[..., DV:]
    attn = jnp.einsum('bhncd,bhnsd->bhncs', q, k) * decay
    qg = q * jnp.exp(gc)[..., None]
    kend = k * jnp.exp(gc[..., -1:] - gc)[..., None]
    glast = jnp.exp(gc[..., -1])
    xs = tuple(jnp.moveaxis(a_, 2, 0) for a_ in (qg, kend, u, w, attn, glast))

    def step(s, inp):
        qg_n, kend_n, u_n, w_n, attn_n, gl_n = inp
        v_new = u_n - jnp.einsum('bhcd,bhde->bhce', w_n, s)
        o = jnp.einsum('bhcd,bhde->bhce', qg_n, s) + jnp.einsum('bhcs,bhse->bhce', attn_n, v_new)
        s = s * gl_n[..., None, None] + jnp.einsum('bhcd,bhce->bhde', kend_n, v_new)
        return s, o

    s, o = lax.scan(step, s0, xs)
    o = jnp.moveaxis(jnp.moveaxis(o, 0, 2), 1, 3).reshape(bsz, t, h, DV)
    return o, s


def _delta_mixer(h, s0, w_in, w_conv, a_log, dt_bias, norm_w, w_out, row_len):
    bsz, t, _ = h.shape
    proj = h @ w_in
    qkv = jax.nn.silu(_short_conv(proj[..., :CONV_CH], w_conv, row_len))
    z = proj[..., CONV_CH:CONV_CH + V_DIM].astype(jnp.float32).reshape(bsz, t, HV, DV)
    ba = proj[..., CONV_CH + V_DIM:].astype(jnp.float32).reshape(bsz, t, 2, 2, HV)
    q = _l2norm(qkv[..., :QK_DIM].reshape(bsz, t, HK, DK)) * (DK ** -0.5)
    k = _l2norm(qkv[..., QK_DIM:2 * QK_DIM].reshape(bsz, t, HK, DK))
    v = qkv[..., 2 * QK_DIM:].astype(jnp.float32).reshape(bsz, t, HV, DV)
    q = jnp.repeat(q, HV // HK, axis=2)
    k = jnp.repeat(k, HV // HK, axis=2)
    beta = jax.nn.sigmoid(ba[:, :, 0])
    g = -jnp.exp(a_log.astype(jnp.float32)) * jax.nn.softplus(ba[:, :, 1] + dt_bias.astype(jnp.float32))
    s0 = s0.astype(jnp.float32)
    o_f, s_f = _chunk_gated_delta(q, k, v, beta[:, :, 0], g[:, :, 0], s0[:, 0])
    rev = lambda a_: jnp.flip(a_, axis=1)
    o_b, s_b = _chunk_gated_delta(rev(q), rev(k), rev(v), rev(beta[:, :, 1]), rev(g[:, :, 1]), s0[:, 1])
    o = o_f + rev(o_b)
    o = _rmsnorm(o, norm_w) * jax.nn.silu(z)
    return o.reshape(bsz, t, V_DIM).astype(h.dtype) @ w_out, jnp.stack([s_f, s_b], axis=1)


def _moe(x, router_w, router_b, w_gu, b_gu, w_down, b_down):
    shp = x.shape
    xt = x.reshape(-1, shp[-1])
    t = xt.shape[0]
    tk = t * TOP_K
    logits = (xt @ router_w).astype(jnp.float32) + router_b.astype(jnp.float32)
    top_v, top_i = lax.top_k(logits, TOP_K)
    gates = jax.nn.softmax(top_v, axis=-1)
    flat_e = top_i.reshape(-1)
    order = jnp.argsort(flat_e)
    se = flat_e[order]
    st = order // TOP_K
    sg = gates.reshape(-1)[order]
    counts = jnp.bincount(flat_e, length=N_EXPERTS)
    padded = (counts + EXPERT_BLOCK - 1) // EXPERT_BLOCK * EXPERT_BLOCK
    pend = jnp.cumsum(padded)
    pstart = pend - padded
    sstart = jnp.cumsum(counts) - counts
    dest = pstart[se] + jnp.arange(tk) - sstart[se]
    n_blocks = (tk + N_EXPERTS * (EXPERT_BLOCK - 1)) // EXPERT_BLOCK
    slot_tok = jnp.full((n_blocks * EXPERT_BLOCK,), t, dtype=jnp.int32).at[dest].set(st.astype(jnp.int32))
    block_e = jnp.minimum(jnp.searchsorted(pend, jnp.arange(n_blocks) * EXPERT_BLOCK, side='right'), N_EXPERTS - 1)
    x_pad = jnp.concatenate([xt, jnp.zeros((1, shp[-1]), xt.dtype)], axis=0)

    def block(args):
        tok, e = args
        hh = x_pad[tok] @ w_gu[e] + b_gu[e]
        hg, hl = jnp.split(hh, 2, axis=-1)
        hg = jnp.minimum(hg, SWIGLU_LIMIT)
        hl = jnp.clip(hl, -SWIGLU_LIMIT, SWIGLU_LIMIT)
        act = hg * jax.nn.sigmoid(SWIGLU_ALPHA * hg) * (hl + 1.0)
        return act @ w_down[e] + b_down[e]

    y_slots = lax.map(block, (slot_tok.reshape(n_blocks, EXPERT_BLOCK), block_e)).reshape(-1, shp[-1])
    y = jax.ops.segment_sum(y_slots[dest] * sg[:, None].astype(y_slots.dtype), st, num_segments=t)
    return y.reshape(shp).astype(x.dtype)


def _stream(x, cond, s_init, row_len, ada_w, ada_b, norm1_w, norm2_w, conv_in_w, conv_w, conv_out_w,
            dn_in_w, dn_conv_w, dn_a_log, dn_dt_bias, dn_norm_w, dn_out_w, router_w, router_b,
            exp_gu_w, exp_gu_b, exp_down_w, exp_down_b, final_norm_w):
    states = []
    for l in range(DEPTH):
        mod = jax.nn.silu(cond) @ ada_w[l] + ada_b[l]
        sh1, sc1, g1, sh2, sc2, g2 = jnp.split(mod[:, None, :], 6, axis=-1)
        h = _rmsnorm(x, norm1_w[l]) * (1.0 + sc1) + sh1
        if l % N_MIXERS == 0:
            i = l // N_MIXERS
            mix = _conv_mixer(h, conv_in_w[i], conv_w[i], conv_out_w[i], row_len)
        else:
            i = l // N_MIXERS
            mix, s = _delta_mixer(h, s_init[:, i], dn_in_w[i], dn_conv_w[i], dn_a_log[i], dn_dt_bias[i],
                                  dn_norm_w[i], dn_out_w[i], row_len)
            states.append(s)
        x = x + g1 * mix
        h = _rmsnorm(x, norm2_w[l]) * (1.0 + sc2) + sh2
        x = x + g2 * _moe(h, router_w[l], router_b[l], exp_gu_w[l], exp_gu_b[l], exp_down_w[l], exp_down_b[l])
    return _rmsnorm(x, final_norm_w), jnp.stack(states, axis=1)


def setup_inputs(seed: int = 0) -> dict:
    key = jax.random.key(seed)
    ks = jax.random.split(key, 26)
    f32 = jnp.float32
    nrm = lambda k_, shape, s: jax.random.normal(k_, shape, f32) * s
    D = D_MODEL
    dt = jnp.exp(jax.random.uniform(ks[13], (N_DELTA_LAYERS, 2, HV), f32, math.log(1e-3), math.log(1e-1)))
    return {
        "x_prompt": nrm(ks[0], (BATCH, SEQ, D), 1.0),
        "x_sample": nrm(ks[1], (DEC_BATCH, DEC_SEQ, D), 1.0),
        "state_delta": nrm(ks[2], (DEC_BATCH, N_DELTA_LAYERS, 2, HV, DK, DV), DK ** -0.5),
        "c": nrm(ks[3], (DEC_BATCH, D), 1.0),
        "c_ctx": nrm(ks[4], (D,), 1.0),
        "ada_w": nrm(ks[5], (DEPTH, D, 6 * D), D ** -0.5),
        "ada_b": nrm(ks[6], (DEPTH, 6 * D), 0.01),
        "norm1_w": 1.0 + nrm(ks[7], (DEPTH, D), 0.02),
        "norm2_w": 1.0 + nrm(ks[8], (DEPTH, D), 0.02),
        "conv_in_w": nrm(ks[9], (N_CONV_LAYERS, D, 3 * D), D ** -0.5),
        "conv_w": nrm(ks[10], (N_CONV_LAYERS, CONV_K, D), CONV_K ** -0.5),
        "conv_out_w": nrm(ks[11], (N_CONV_LAYERS, D, D), D ** -0.5),
        "dn_in_w": nrm(ks[12], (N_DELTA_LAYERS, D, DN_IN), D ** -0.5),
        "dn_conv_w": nrm(ks[14], (N_DELTA_LAYERS, CONV_K, CONV_CH), CONV_K ** -0.5),
        "dn_a_log": jnp.log(jax.random.uniform(ks[15], (N_DELTA_LAYERS, 2, HV), f32, 1.0, 16.0)),
        "dn_dt_bias": jnp.log(jnp.expm1(dt)),
        "dn_norm_w": 1.0 + nrm(ks[16], (N_DELTA_LAYERS, DV), 0.02),
        "dn_out_w": nrm(ks[17], (N_DELTA_LAYERS, V_DIM, D), V_DIM ** -0.5),
        "router_w": nrm(ks[18], (DEPTH, D, N_EXPERTS), D ** -0.5),
        "router_b": nrm(ks[19], (DEPTH, N_EXPERTS), 0.01),
        "exp_gu_w": nrm(ks[20], (DEPTH, N_EXPERTS, D, 2 * D_FF), D ** -0.5),
        "exp_gu_b": nrm(ks[21], (DEPTH, N_EXPERTS, 2 * D_FF), 0.01),
        "exp_down_w": nrm(ks[22], (DEPTH, N_EXPERTS, D_FF, D), D_FF ** -0.5),
        "exp_down_b": nrm(ks[23], (DEPTH, N_EXPERTS, D), 0.01),
        "final_norm_w": 1.0 + nrm(ks[24], (D,), 0.02),
    }


def reference(x_prompt, x_sample, state_delta, c, c_ctx, ada_w, ada_b, norm1_w, norm2_w, conv_in_w, conv_w,
              conv_out_w, dn_in_w, dn_conv_w, dn_a_log, dn_dt_bias, dn_norm_w, dn_out_w, router_w, router_b,
              exp_gu_w, exp_gu_b, exp_down_w, exp_down_b, final_norm_w):
    weights = (ada_w, ada_b, norm1_w, norm2_w, conv_in_w, conv_w, conv_out_w, dn_in_w, dn_conv_w, dn_a_log,
               dn_dt_bias, dn_norm_w, dn_out_w, router_w, router_b, exp_gu_w, exp_gu_b, exp_down_w,
               exp_down_b, final_norm_w)
    ctx_len = x_prompt.shape[1]
    s_zero = jnp.zeros((x_prompt.shape[0], N_DELTA_LAYERS, 2, HV, DK, DV), jnp.float32)
    y_prompt, new_state_delta = _stream(x_prompt, c_ctx[None, :], s_zero, ctx_len, *weights)
    y_sample, _ = _stream(x_sample, c, state_delta, GRID_W, *weights)
    return (y_prompt, y_sample, new_state_delta)
```

```python
import functools

import jax
import jax.numpy as jnp
from jax import lax
from jax.experimental import pallas as pl
from jax.experimental.pallas import tpu as pltpu

F32 = jnp.float32
BF16 = jnp.bfloat16

D = 1024
TM = 256
CTX_ROW = 256
GRID_W = 64
HK, HV, DK, DV = 8, 16, 128, 128
QK_DIM = HK * DK
V_DIM = HV * DV
CONV_CH = 2 * QK_DIM + V_DIM
CHUNK = 64
N_EXPERTS = 32
TOP_K = 4
D_FF = 1024
SWIGLU_ALPHA = 1.702
SWIGLU_LIMIT = 7.0
EPS = 1e-6
LANES = 128
N_COND = 8
BLK = 256
NEG = -3.0e38
VMEM_LIMIT = 56 * 1024 * 1024
HIGHEST = lax.Precision.HIGHEST
G0 = 2 * HV


def _silu(x):
    return x * jax.nn.sigmoid(x)


def _bdot(a, b):
    return jnp.dot(a.astype(BF16), b.astype(BF16), preferred_element_type=F32)


def _params(sem):
    return pltpu.CompilerParams(dimension_semantics=sem, vmem_limit_bytes=VMEM_LIMIT)


def _ada_body(c_ref, w_ref, b_ref, o_ref):
    o_ref[...] = _bdot(_silu(c_ref[...]), w_ref[...]) + b_ref[...]


def _ada(cond, ada_w, ada_b):
    n_layers = ada_w.shape[0]
    tn = 1024
    out = pl.pallas_call(
        _ada_body,
        grid=(n_layers, 6 * D // tn),
        in_specs=[
            pl.BlockSpec((N_COND, D), lambda l, j: (0, 0)),
            pl.BlockSpec((None, D, tn), lambda l, j: (l, 0, j)),
            pl.BlockSpec((None, 1, tn), lambda l, j: (l, 0, j)),
        ],
        out_specs=pl.BlockSpec((None, N_COND, tn), lambda l, j: (l, 0, j)),
        out_shape=jax.ShapeDtypeStruct((n_layers, N_COND, 6 * D), F32),
        compiler_params=_params(("arbitrary", "arbitrary")),
        name="ada",
    )(cond, ada_w, ada_b.reshape(n_layers, 1, 6 * D))
    return out.reshape(n_layers, N_COND, 6, D)


def _prenorm(x, nw, sc, sh):
    ms = jnp.mean(x * x, axis=-1, keepdims=True)
    return (x * lax.rsqrt(ms + EPS) * nw) * (1.0 + sc) + sh


def _short_conv(v, w_ref, row_len):
    n = v.shape[0]
    pos = lax.broadcasted_iota(jnp.int32, (n, 1), 0) & (row_len - 1)
    prev = jnp.where(pos == 0, 0.0, pltpu.roll(v, 1, 0))
    nxt = jnp.where(pos == row_len - 1, 0.0, pltpu.roll(v, n - 1, 0))
    return prev * w_ref[0:1, :] + v * w_ref[1:2, :] + nxt * w_ref[2:3, :]


def _route(h2, rw_ref, rb_ref, carry_ref, idx_ref, rank_ref, gate_ref, cnt_ref):
    logits = jnp.dot(h2, rw_ref[...], preferred_element_type=F32, precision=HIGHEST) + rb_ref[...]
    lane = lax.broadcasted_iota(jnp.int32, (TM, LANES), 1)
    work = logits
    sel = jnp.zeros((TM, LANES), F32)
    vals, ids, hots = [], [], []
    for _ in range(TOP_K):
        m = jnp.max(work, axis=-1, keepdims=True)
        ik = jnp.min(jnp.where(work == m, lane, LANES), axis=-1, keepdims=True)
        hot = lane == ik
        work = jnp.where(hot, NEG, work)
        sel = sel + hot.astype(F32)
        vals.append(m)
        ids.append(ik)
        hots.append(hot)
    es = [jnp.exp(v - vals[0]) for v in vals]
    den = es[0] + es[1] + es[2] + es[3]
    r = lax.broadcasted_iota(jnp.int32, (TM, TM), 0)
    c = lax.broadcasted_iota(jnp.int32, (TM, TM), 1)
    before = (c < r).astype(BF16)
    rank_all = jnp.dot(before, sel.astype(BF16), preferred_element_type=F32) + carry_ref[0:1, :]
    idx_o = jnp.zeros((TM, LANES), jnp.int32)
    rank_o = jnp.zeros((TM, LANES), jnp.int32)
    gate_o = jnp.zeros((TM, LANES), F32)
    for k in range(TOP_K):
        rk = jnp.sum(jnp.where(hots[k], rank_all, 0.0), axis=-1, keepdims=True)
        idx_o = jnp.where(lane == k, ids[k], idx_o)
        rank_o = jnp.where(lane == k, rk.astype(jnp.int32), rank_o)
        gate_o = jnp.where(lane == k, es[k] / den, gate_o)
    idx_ref[...] = idx_o
    rank_ref[...] = rank_o
    gate_ref[...] = gate_o
    total = carry_ref[...] + jnp.sum(sel, axis=0, keepdims=True)
    carry_ref[...] = total
    cnt_ref[...] = total


def _tile_maps(n_ctx_tiles, tiles_per_seq):
    def grp(i):
        return jnp.where(i < n_ctx_tiles, 0, 1 + (i - n_ctx_tiles) // tiles_per_seq)

    tok = lambda i: (i, 0)
    const = lambda i: (0, 0)
    mod = lambda i: (grp(i), 0, 0)
    return tok, const, mod


def _route_specs(const):
    return [
        pl.BlockSpec((1, D), const),
        pl.BlockSpec((D, LANES), const),
        pl.BlockSpec((1, LANES), const),
    ]


def _route_outs(n_tok, tok, const):
    specs = [
        pl.BlockSpec((TM, D), tok),
        pl.BlockSpec((TM, D), tok),
        pl.BlockSpec((TM, LANES), tok),
        pl.BlockSpec((TM, LANES), tok),
        pl.BlockSpec((TM, LANES), tok),
        pl.BlockSpec((8, LANES), const),
    ]
    shapes = [
        jax.ShapeDtypeStruct((n_tok, D), F32),
        jax.ShapeDtypeStruct((n_tok, D), F32),
        jax.ShapeDtypeStruct((n_tok, LANES), jnp.int32),
        jax.ShapeDtypeStruct((n_tok, LANES), jnp.int32),
        jax.ShapeDtypeStruct((n_tok, LANES), F32),
        jax.ShapeDtypeStruct((8, LANES), F32),
    ]
    return specs, shapes


def _moe_front(x1, mod_ref, n2_ref, rw_ref, rb_ref, carry_ref, x1_ref, h2_ref, idx_ref, rank_ref, gate_ref,
               cnt_ref):
    @pl.when(pl.program_id(0) == 0)
    def _():
        carry_ref[...] = jnp.zeros_like(carry_ref)

    x1_ref[...] = x1
    h2 = _prenorm(x1, n2_ref[...], mod_ref[4:5, :], mod_ref[3:4, :])
    h2_ref[...] = h2
    _route(h2, rw_ref, rb_ref, carry_ref, idx_ref, rank_ref, gate_ref, cnt_ref)


def _conv_body(n_ctx_tiles, x_ref, mod_ref, n1_ref, win_ref, cw_ref, wout_ref, n2_ref, rw_ref, rb_ref,
               x1_ref, h2_ref, idx_ref, rank_ref, gate_ref, cnt_ref, carry_ref):
    i = pl.program_id(0)
    row_len = jnp.where(i < n_ctx_tiles, CTX_ROW, GRID_W)
    x = x_ref[...]
    h = _prenorm(x, n1_ref[...], mod_ref[1:2, :], mod_ref[0:1, :])
    proj = _bdot(h, win_ref[...])
    b, cg, u = proj[:, :D], proj[:, D:2 * D], proj[:, 2 * D:]
    mix = _bdot(b * _short_conv(cg * u, cw_ref, row_len), wout_ref[...])
    x1 = x + mod_ref[2:3, :] * mix
    _moe_front(x1, mod_ref, n2_ref, rw_ref, rb_ref, carry_ref, x1_ref, h2_ref, idx_ref, rank_ref, gate_ref,
               cnt_ref)


def _conv_layer(x, mod, n1, win, cw, wout, n2, rw, rb, n_ctx_tiles, tiles_per_seq):
    n_tok = x.shape[0]
    tok, const, modm = _tile_maps(n_ctx_tiles, tiles_per_seq)
    out_specs, out_shapes = _route_outs(n_tok, tok, const)
    return pl.pallas_call(
        functools.partial(_conv_body, n_ctx_tiles),
        grid=(n_tok // TM,),
        in_specs=[
            pl.BlockSpec((TM, D), tok),
            pl.BlockSpec((None, 6, D), modm),
            pl.BlockSpec((1, D), const),
            pl.BlockSpec((D, 3 * D), const),
            pl.BlockSpec((3, D), const),
            pl.BlockSpec((D, D), const),
        ] + _route_specs(const),
        out_specs=out_specs,
        out_shape=out_shapes,
        scratch_shapes=[pltpu.VMEM((8, LANES), F32)],
        compiler_params=_params(("arbitrary",)),
        name="conv_layer",
    )(x, mod, n1, win, cw, wout, n2, rw, rb)


def _scatter_body(dest_ref, pend_ref, h_ref, xs_ref, zbuf, sem, zsem):
    i = pl.program_id(0)

    @pl.when(i == 0)
    def _():
        zbuf[...] = jnp.zeros_like(zbuf)

        def zero_block(j):
            return pltpu.make_async_copy(zbuf, xs_ref.at[pl.ds(pl.multiple_of(j * BLK, BLK), BLK)], zsem)

        def start(j, carry):
            zero_block(j).start()
            return carry

        def wait(j, carry):
            zero_block(j).wait()
            return carry

        n_used = pend_ref[N_EXPERTS - 1] // BLK
        lax.fori_loop(n_used, xs_ref.shape[0] // BLK, start, 0)
        lax.fori_loop(n_used, xs_ref.shape[0] // BLK, wait, 0)
        for phase in range(2):
            for e in range(N_EXPERTS):
                end = pend_ref[e]
                start = pend_ref[e - 1] if e else 0

                @pl.when(end > start)
                def _():
                    tail = pl.ds(pl.multiple_of(end - BLK, BLK), BLK)
                    cp = pltpu.make_async_copy(zbuf, xs_ref.at[tail], zsem)
                    if phase == 0:
                        cp.start()
                    else:
                        cp.wait()

    base = i * (TM * TOP_K)

    def row_copy(t, k):
        return pltpu.make_async_copy(h_ref.at[pl.ds(t, 1)], xs_ref.at[pl.ds(dest_ref[base + TOP_K * t + k], 1)],
                                     sem)

    def issue(t, carry):
        for k in range(TOP_K):
            row_copy(t, k).start()
        return carry

    def drain(t, carry):
        for k in range(TOP_K):
            row_copy(t, k).wait()
        return carry

    lax.fori_loop(0, TM, issue, 0)
    lax.fori_loop(0, TM, drain, 0)


def _scatter(dest, pend, h2, n_slots):
    n_tok = h2.shape[0]
    return pl.pallas_call(
        _scatter_body,
        grid_spec=pltpu.PrefetchScalarGridSpec(
            num_scalar_prefetch=2,
            grid=(n_tok // TM,),
            in_specs=[pl.BlockSpec((TM, D), lambda i, *_: (i, 0))],
            out_specs=pl.BlockSpec(memory_space=pl.ANY),
            scratch_shapes=[pltpu.VMEM((BLK, D), F32), pltpu.SemaphoreType.DMA, pltpu.SemaphoreType.DMA],
        ),
        out_shape=jax.ShapeDtypeStruct((n_slots, D), F32),
        compiler_params=_params(("arbitrary",)),
        name="moe_scatter",
    )(dest, pend, h2)


W_ROWS = 128


def _expert_body(be_ref, nu_ref, x_ref, wgu_ref, bgu_ref, wd_ref, bd_ref, y_ref, wgu_bf, wd_bf):
    j = pl.program_id(0)
    n_used = nu_ref[0]
    jj = jnp.minimum(j, n_used - 1)
    e = be_ref[jj]
    e_prev = be_ref[jnp.maximum(jj - 1, 0)]

    @pl.when((j == 0) | (e != e_prev))
    def _():
        def cast(r, carry):
            rows = pl.ds(pl.multiple_of(r * W_ROWS, W_ROWS), W_ROWS)
            wgu_bf[rows, :] = wgu_ref[rows, :].astype(BF16)
            wd_bf[rows, :] = wd_ref[rows, :].astype(BF16)
            return carry

        lax.fori_loop(0, D // W_ROWS, cast, 0)

    @pl.when(j < n_used)
    def _():
        hh = jnp.dot(x_ref[...].astype(BF16), wgu_bf[...], preferred_element_type=F32) + bgu_ref[...]
        hg = jnp.minimum(hh[:, :D_FF], SWIGLU_LIMIT)
        hl = jnp.clip(hh[:, D_FF:], -SWIGLU_LIMIT, SWIGLU_LIMIT)
        act = hg * jax.nn.sigmoid(SWIGLU_ALPHA * hg) * (hl + 1.0)
        y_ref[...] = jnp.dot(act.astype(BF16), wd_bf[...], preferred_element_type=F32) + bd_ref[...]

    @pl.when(j >= n_used)
    def _():
        y_ref[...] = jnp.zeros_like(y_ref)


def _experts(block_e, n_used, xs, wgu, bgu, wd, bd):
    n_slots = xs.shape[0]

    def blk(j, be, nu):
        return (jnp.minimum(j, nu[0] - 1), 0)

    def wmap(j, be, nu):
        return (be[jnp.minimum(j, nu[0] - 1)], 0, 0)

    return pl.pallas_call(
        _expert_body,
        grid_spec=pltpu.PrefetchScalarGridSpec(
            num_scalar_prefetch=2,
            grid=(n_slots // BLK,),
            in_specs=[
                pl.BlockSpec((BLK, D), blk),
                pl.BlockSpec((None, D, 2 * D_FF), wmap),
                pl.BlockSpec((None, 1, 2 * D_FF), wmap),
                pl.BlockSpec((None, D_FF, D), wmap),
                pl.BlockSpec((None, 1, D), wmap),
            ],
            out_specs=pl.BlockSpec((BLK, D), lambda j, be, nu: (j, 0)),
            scratch_shapes=[pltpu.VMEM((D, 2 * D_FF), BF16), pltpu.VMEM((D_FF, D), BF16)],
        ),
        out_shape=jax.ShapeDtypeStruct((n_slots, D), F32),
        compiler_params=_params(("arbitrary",)),
        name="moe_experts",
    )(block_e, n_used, xs, wgu, bgu.reshape(N_EXPERTS, 1, 2 * D_FF), wd, bd.reshape(N_EXPERTS, 1, D))


def _combine_body(final, dest_ref, x_ref, gate_ref, mod_ref, fnw_ref, y_ref, o_ref, buf, sem):
    i = pl.program_id(0)
    base = i * (TM * TOP_K)

    def row_copy(t, k):
        return pltpu.make_async_copy(y_ref.at[pl.ds(dest_ref[base + TOP_K * t + k], 1)],
                                     buf.at[k, pl.ds(t, 1)], sem)

    def issue(t, carry):
        for k in range(TOP_K):
            row_copy(t, k).start()
        return carry

    def drain(t, carry):
        for k in range(TOP_K):
            row_copy(t, k).wait()
        return carry

    lax.fori_loop(0, TM, issue, 0)
    lax.fori_loop(0, TM, drain, 0)
    gate = gate_ref[...]
    acc = gate[:, 0:1] * buf[0]
    for k in range(1, TOP_K):
        acc = acc + gate[:, k:k + 1] * buf[k]
    x2 = x_ref[...] + mod_ref[5:6, :] * acc
    if final:
        ms = jnp.mean(x2 * x2, axis=-1, keepdims=True)
        x2 = x2 * lax.rsqrt(ms + EPS) * fnw_ref[...]
    o_ref[...] = x2


def _combine(dest, x1, gate, mod, fnw, y_slots, n_ctx_tiles, tiles_per_seq, final):
    n_tok = x1.shape[0]

    def grp(i):
        return jnp.where(i < n_ctx_tiles, 0, 1 + (i - n_ctx_tiles) // tiles_per_seq)

    return pl.pallas_call(
        functools.partial(_combine_body, final),
        grid_spec=pltpu.PrefetchScalarGridSpec(
            num_scalar_prefetch=1,
            grid=(n_tok // TM,),
            in_specs=[
                pl.BlockSpec((TM, D), lambda i, d: (i, 0)),
                pl.BlockSpec((TM, LANES), lambda i, d: (i, 0)),
                pl.BlockSpec((None, 6, D), lambda i, d: (grp(i), 0, 0)),
                pl.BlockSpec((1, D), lambda i, d: (0, 0)),
                pl.BlockSpec(memory_space=pl.ANY),
            ],
            out_specs=pl.BlockSpec((TM, D), lambda i, d: (i, 0)),
            scratch_shapes=[pltpu.VMEM((TOP_K, TM, D), F32), pltpu.SemaphoreType.DMA],
        ),
        out_shape=jax.ShapeDtypeStruct((n_tok, D), F32),
        compiler_params=_params(("arbitrary",)),
        name="moe_combine",
    )(dest, x1, gate, mod, fnw, y_slots)


def _moe(x1, h2, idx, rank, gate, counts, mod, fnw, wgu, bgu, wd, bd, n_ctx_tiles, tiles_per_seq, final):
    n_tok = x1.shape[0]
    n_slots = n_tok * TOP_K + N_EXPERTS * BLK
    cnt = counts[0, :N_EXPERTS].astype(jnp.int32)
    padded = (cnt + BLK - 1) // BLK * BLK
    pend = jnp.cumsum(padded)
    pstart = pend - padded
    dest = (pstart[idx[:, :TOP_K]] + rank[:, :TOP_K]).reshape(-1).astype(jnp.int32)
    n_blocks = n_slots // BLK
    block_e = jnp.minimum(jnp.searchsorted(pend, jnp.arange(n_blocks) * BLK, side="right"),
                          N_EXPERTS - 1).astype(jnp.int32)
    n_used = (pend[-1:] // BLK).astype(jnp.int32)
    xs = _scatter(dest, pend.astype(jnp.int32), h2, n_slots)
    ys = _experts(block_e, n_used, xs, wgu, bgu, wd, bd)
    return _combine(dest, x1, gate, mod, fnw, ys, n_ctx_tiles, tiles_per_seq, final)


def _dn_in_body(n_ctx_tiles, x_ref, mod_ref, n1_ref, w_ref, wba_ref, cw_ref, alog_ref, dtb_ref,
                q_ref, k_ref, v_ref, z_ref, bg_ref):
    i = pl.program_id(0)
    row_len = jnp.where(i < n_ctx_tiles, CTX_ROW, GRID_W)
    h = _prenorm(x_ref[...], n1_ref[...], mod_ref[1:2, :], mod_ref[0:1, :]).astype(BF16)
    proj = jnp.dot(h, w_ref[...], preferred_element_type=F32)
    z_ref[...] = proj[:, CONV_CH:]
    qkv = _silu(_short_conv(proj[:, :CONV_CH], cw_ref, row_len))
    for hd in range(2 * HK):
        s = qkv[:, hd * DK:(hd + 1) * DK]
        n = s * lax.rsqrt(jnp.sum(s * s, axis=-1, keepdims=True) + EPS)
        if hd < HK:
            q_ref[:, hd * DK:(hd + 1) * DK] = n * (DK ** -0.5)
        else:
            k_ref[:, (hd - HK) * DK:(hd - HK + 1) * DK] = n
    v_ref[...] = qkv[:, 2 * QK_DIM:]
    ba = jnp.dot(h, wba_ref[...], preferred_element_type=F32)
    beta = jax.nn.sigmoid(ba)
    a = ba + dtb_ref[...]
    softplus = jnp.maximum(a, 0.0) + jnp.log(1.0 + jnp.exp(-jnp.abs(a)))
    g = -jnp.exp(alog_ref[...]) * softplus
    lane = lax.broadcasted_iota(jnp.int32, (TM, LANES), 1)
    bg_ref[...] = jnp.where(lane < G0, beta, g)


def _dn_in(x, mod, n1, w, wba, cw, alog, dtb, n_ctx_tiles, tiles_per_seq):
    n_tok = x.shape[0]
    tok, const, modm = _tile_maps(n_ctx_tiles, tiles_per_seq)
    return pl.pallas_call(
        functools.partial(_dn_in_body, n_ctx_tiles),
        grid=(n_tok // TM,),
        in_specs=[
            pl.BlockSpec((TM, D), tok),
            pl.BlockSpec((None, 6, D), modm),
            pl.BlockSpec((1, D), const),
            pl.BlockSpec((D, CONV_CH + V_DIM), const),
            pl.BlockSpec((D, LANES), const),
            pl.BlockSpec((3, CONV_CH), const),
            pl.BlockSpec((1, LANES), const),
            pl.BlockSpec((1, LANES), const),
        ],
        out_specs=[
            pl.BlockSpec((TM, QK_DIM), tok),
            pl.BlockSpec((TM, QK_DIM), tok),
            pl.BlockSpec((TM, V_DIM), tok),
            pl.BlockSpec((TM, V_DIM), tok),
            pl.BlockSpec((TM, LANES), tok),
        ],
        out_shape=[
            jax.ShapeDtypeStruct((n_tok, QK_DIM), F32),
            jax.ShapeDtypeStruct((n_tok, QK_DIM), F32),
            jax.ShapeDtypeStruct((n_tok, V_DIM), F32),
            jax.ShapeDtypeStruct((n_tok, V_DIM), F32),
            jax.ShapeDtypeStruct((n_tok, LANES), F32),
        ],
        compiler_params=_params(("arbitrary",)),
        name="dn_in",
    )(x, mod, n1, w, wba, cw, alog, dtb)


def _hdot(a, b):
    return jnp.dot(a, b, preferred_element_type=F32, precision=HIGHEST)


def _delta_body(q_ref, k_ref, v_ref, bg_ref, s0_ref, o_ref, sout_ref, state):
    c = pl.program_id(1)
    n_chunks = pl.num_programs(1)
    bwd = (pl.program_id(0) % 2) == 1

    @pl.when(c == 0)
    def _():
        state[...] = s0_ref[...]

    row = lax.broadcasted_iota(jnp.int32, (CHUNK, CHUNK), 0)
    col = lax.broadcasted_iota(jnp.int32, (CHUNK, CHUNK), 1)
    ahead = (row - col) * jnp.where(bwd, -1, 1)
    causal = ahead >= 0
    strict = ahead > 0
    eye = (row == col).astype(F32)
    bg = jnp.where(bwd, pltpu.roll(bg_ref[...], LANES - HV, 1), bg_ref[...])
    gc = _hdot(causal.astype(F32), bg)
    gct = gc.T
    gtot = jnp.sum(bg, axis=0, keepdims=True)
    egc = jnp.exp(gc)
    eend = jnp.exp(gtot - gc)
    etot = jnp.exp(gtot)
    kkt = qkt = None
    for h in range(HV):
        hk = h // (HV // HK)
        kh = k_ref[:, hk * DK:(hk + 1) * DK]
        qh = q_ref[:, hk * DK:(hk + 1) * DK]
        vh = v_ref[:, h * DV:(h + 1) * DV]
        if h % (HV // HK) == 0:
            kb16 = kh.astype(BF16)
            kkt = lax.dot_general(kb16, kb16, (((1,), (1,)), ((), ())), preferred_element_type=F32)
            qkt = lax.dot_general(qh.astype(BF16), kb16, (((1,), (1,)), ((), ())), preferred_element_type=F32)
        beta = bg[:, h:h + 1]
        gcol = gc[:, G0 + h:G0 + h + 1]
        grow = gct[G0 + h:G0 + h + 1, :]
        decay = jnp.exp(jnp.where(causal, gcol - grow, NEG))
        x = jnp.where(strict, -(kkt * decay * beta), 0.0)
        tinv = eye + x
        for _ in range(5):
            x = _hdot(x, x)
            tinv = tinv + _hdot(tinv, x)
        rhs = jnp.concatenate([vh * beta, kh * (beta * egc[:, G0 + h:G0 + h + 1])], axis=1)
        sol = _hdot(tinv, rhs)
        u, w = sol[:, :DV], sol[:, DV:]
        attn = qkt * decay
        qg = qh * egc[:, G0 + h:G0 + h + 1]
        kend = kh * eend[:, G0 + h:G0 + h + 1]
        s = state[h]
        v_new = u - _bdot(w, s)
        o_ref[:, h * DV:(h + 1) * DV] = _bdot(qg, s) + _bdot(attn, v_new)
        state[h] = s * etot[:, G0 + h:G0 + h + 1] + lax.dot_general(
            kend.astype(BF16), v_new.astype(BF16), (((0,), (0,)), ((), ())), preferred_element_type=F32)

    @pl.when(c == n_chunks - 1)
    def _():
        sout_ref[...] = state[...]


def _delta(q, k, v, bg, s0, blk0, n_seq, n_chunks):
    n_rows = n_seq * n_chunks * CHUNK

    def tokblk(sd, c):
        cc = jnp.where(sd % 2 == 1, n_chunks - 1 - c, c)
        return (sd // 2) * n_chunks + cc

    return pl.pallas_call(
        _delta_body,
        grid=(2 * n_seq, n_chunks),
        in_specs=[
            pl.BlockSpec((CHUNK, QK_DIM), lambda sd, c: (blk0 + tokblk(sd, c), 0)),
            pl.BlockSpec((CHUNK, QK_DIM), lambda sd, c: (blk0 + tokblk(sd, c), 0)),
            pl.BlockSpec((CHUNK, V_DIM), lambda sd, c: (blk0 + tokblk(sd, c), 0)),
            pl.BlockSpec((CHUNK, LANES), lambda sd, c: (blk0 + tokblk(sd, c), 0)),
            pl.BlockSpec((None, None, HV, DK, DV), lambda sd, c: (sd // 2, sd % 2, 0, 0, 0)),
        ],
        out_specs=[
            pl.BlockSpec((None, CHUNK, V_DIM), lambda sd, c: (sd % 2, tokblk(sd, c), 0)),
            pl.BlockSpec((None, None, HV, DK, DV), lambda sd, c: (sd // 2, sd % 2, 0, 0, 0)),
        ],
        out_shape=[
            jax.ShapeDtypeStruct((2, n_rows, V_DIM), F32),
            jax.ShapeDtypeStruct((n_seq, 2, HV, DK, DV), F32),
        ],
        scratch_shapes=[pltpu.VMEM((HV, DK, DV), F32)],
        compiler_params=_params(("arbitrary", "arbitrary")),
        name="delta_rule",
    )(q, k, v, bg, s0)


def _dn_out_body(of_ref, ob_ref, z_ref, x_ref, mod_ref, nw_ref, wout_ref, n2_ref, rw_ref, rb_ref,
                 x1_ref, h2_ref, idx_ref, rank_ref, gate_ref, cnt_ref, carry_ref):
    o = of_ref[...] + ob_ref[...]
    z = z_ref[...]
    parts = []
    for h in range(HV):
        oh = o[:, h * DV:(h + 1) * DV]
        nh = oh * lax.rsqrt(jnp.mean(oh * oh, axis=-1, keepdims=True) + EPS) * nw_ref[...]
        parts.append((nh * _silu(z[:, h * DV:(h + 1) * DV])).astype(BF16))
    mix = jnp.dot(jnp.concatenate(parts, axis=1), wout_ref[...], preferred_element_type=F32)
    x1 = x_ref[...] + mod_ref[2:3, :] * mix
    _moe_front(x1, mod_ref, n2_ref, rw_ref, rb_ref, carry_ref, x1_ref, h2_ref, idx_ref, rank_ref, gate_ref,
               cnt_ref)


def _dn_out(o, z, x, mod, nw, wout, n2, rw, rb, n_ctx_tiles, tiles_per_seq):
    n_tok = x.shape[0]
    tok, const, modm = _tile_maps(n_ctx_tiles, tiles_per_seq)
    out_specs, out_shapes = _route_outs(n_tok, tok, const)
    return pl.pallas_call(
        _dn_out_body,
        grid=(n_tok // TM,),
        in_specs=[
            pl.BlockSpec((None, TM, V_DIM), lambda i: (0, i, 0)),
            pl.BlockSpec((None, TM, V_DIM), lambda i: (1, i, 0)),
            pl.BlockSpec((TM, V_DIM), tok),
            pl.BlockSpec((TM, D), tok),
            pl.BlockSpec((None, 6, D), modm),
            pl.BlockSpec((1, DV), const),
            pl.BlockSpec((V_DIM, D), const),
        ] + _route_specs(const),
        out_specs=out_specs,
        out_shape=out_shapes,
        scratch_shapes=[pltpu.VMEM((8, LANES), F32)],
        compiler_params=_params(("arbitrary",)),
        name="dn_out",
    )(o, o, z, x, mod, nw, wout, n2, rw, rb)


def kernel(x_prompt, x_sample, state_delta, c, c_ctx, ada_w, ada_b, norm1_w, norm2_w, conv_in_w, conv_w,
           conv_out_w, dn_in_w, dn_conv_w, dn_a_log, dn_dt_bias, dn_norm_w, dn_out_w, router_w, router_b,
           exp_gu_w, exp_gu_b, exp_down_w, exp_down_b, final_norm_w):
    n_ctx, ctx_len, _ = x_prompt.shape
    n_dec, dec_len, _ = x_sample.shape
    assert ctx_len == CTX_ROW and dec_len % TM == 0 and n_dec + 1 <= N_COND
    n_ctx_tok = n_ctx * ctx_len
    n_ctx_tiles = n_ctx_tok // TM
    tiles_per_seq = dec_len // TM

    x = jnp.concatenate([x_prompt.reshape(n_ctx_tok, D), x_sample.reshape(n_dec * dec_len, D)], axis=0)
    cond = jnp.zeros((N_COND, D), F32).at[0].set(c_ctx).at[1:1 + n_dec].set(c)
    mod = _ada(cond, ada_w, ada_b)

    def router(l):
        rw = jnp.zeros((D, LANES), F32).at[:, :N_EXPERTS].set(router_w[l])
        rb = jnp.full((1, LANES), NEG, F32).at[0, :N_EXPERTS].set(router_b[l])
        return norm2_w[l][None, :], rw, rb

    def experts(l):
        return exp_gu_w[l], exp_gu_b[l], exp_down_w[l], exp_down_b[l]

    front = _conv_layer(x, mod[0], norm1_w[0][None, :], conv_in_w[0].astype(BF16), conv_w[0],
                        conv_out_w[0].astype(BF16), *router(0), n_ctx_tiles, tiles_per_seq)
    x = _moe(*front, mod[0], final_norm_w[None, :], *experts(0), n_ctx_tiles, tiles_per_seq, False)

    w_in = dn_in_w[0]
    w_ba = jnp.zeros((D, LANES), F32).at[:, :4 * HV].set(w_in[:, CONV_CH + V_DIM:]).astype(BF16)
    lane_pad = lambda a: jnp.zeros((1, LANES), F32).at[0, 2 * HV:4 * HV].set(a.reshape(-1))
    q, k, v, z, bg = _dn_in(x, mod[1], norm1_w[1][None, :], w_in[:, :CONV_CH + V_DIM].astype(BF16), w_ba,
                            dn_conv_w[0], lane_pad(dn_a_log[0]), lane_pad(dn_dt_bias[0]), n_ctx_tiles,
                            tiles_per_seq)
    s_zero = jnp.zeros((n_ctx, 2, HV, DK, DV), F32)
    o_ctx, s_ctx = _delta(q, k, v, bg, s_zero, 0, n_ctx, ctx_len // CHUNK)
    o_dec, _ = _delta(q, k, v, bg, state_delta[:, 0].astype(F32), n_ctx_tok // CHUNK, n_dec, dec_len // CHUNK)
    o = jnp.concatenate([o_ctx, o_dec], axis=1)
    front = _dn_out(o, z, x, mod[1], dn_norm_w[0][None, :], dn_out_w[0].astype(BF16), *router(1),
                    n_ctx_tiles, tiles_per_seq)
    y = _moe(*front, mod[1], final_norm_w[None, :], *experts(1), n_ctx_tiles, tiles_per_seq, True)

    y_prompt = y[:n_ctx_tok].reshape(n_ctx, ctx_len, D)
    y_sample = y[n_ctx_tok:].reshape(n_dec, dec_len, D)
    return y_prompt, y_sample, s_ctx[:, None]
```

```python
import functools

import jax
import jax.numpy as jnp
from jax import lax
from jax.experimental import pallas as pl
from jax.experimental.pallas import tpu as pltpu

F32 = jnp.float32
BF16 = jnp.bfloat16

D = 1024
TM = 256
CTX_ROW = 256
GRID_W = 64
HK, HV, DK, DV = 8, 16, 128, 128
QK_DIM = HK * DK
V_DIM = HV * DV
CONV_CH = 2 * QK_DIM + V_DIM
CHUNK = 64
N_EXPERTS = 32
TOP_K = 4
D_FF = 1024
SWIGLU_ALPHA = 1.702
SWIGLU_LIMIT = 7.0
EPS = 1e-6
LANES = 128
N_COND = 8
BLK = 256
NEG = -3.0e38
VMEM_LIMIT = 56 * 1024 * 1024
HIGHEST = lax.Precision.HIGHEST
G0 = 2 * HV
PAIR = HV // HK


def _silu(x):
    return x * jax.nn.sigmoid(x)


def _bdot(a, b):
    return jnp.dot(a.astype(BF16), b.astype(BF16), preferred_element_type=F32)


def _params(sem):
    return pltpu.CompilerParams(dimension_semantics=sem, vmem_limit_bytes=VMEM_LIMIT)


def _ada_body(c_ref, w_ref, b_ref, o_ref):
    o_ref[...] = _bdot(_silu(c_ref[...]), w_ref[...]) + b_ref[...]


def _ada(cond, ada_w, ada_b):
    n_layers = ada_w.shape[0]
    tn = 1024
    out = pl.pallas_call(
        _ada_body,
        grid=(n_layers, 6 * D // tn),
        in_specs=[
            pl.BlockSpec((N_COND, D), lambda l, j: (0, 0)),
            pl.BlockSpec((None, D, tn), lambda l, j: (l, 0, j)),
            pl.BlockSpec((None, 1, tn), lambda l, j: (l, 0, j)),
        ],
        out_specs=pl.BlockSpec((None, N_COND, tn), lambda l, j: (l, 0, j)),
        out_shape=jax.ShapeDtypeStruct((n_layers, N_COND, 6 * D), F32),
        compiler_params=_params(("arbitrary", "arbitrary")),
        name="ada",
    )(cond, ada_w, ada_b.reshape(n_layers, 1, 6 * D))
    return out.reshape(n_layers, N_COND, 6, D)


def _prenorm(x, nw, sc, sh):
    ms = jnp.mean(x * x, axis=-1, keepdims=True)
    return (x * lax.rsqrt(ms + EPS) * nw) * (1.0 + sc) + sh


def _short_conv(v, w_ref, row_len):
    n = v.shape[0]
    pos = lax.broadcasted_iota(jnp.int32, (n, 1), 0) & (row_len - 1)
    prev = jnp.where(pos == 0, 0.0, pltpu.roll(v, 1, 0))
    nxt = jnp.where(pos == row_len - 1, 0.0, pltpu.roll(v, n - 1, 0))
    return prev * w_ref[0:1, :] + v * w_ref[1:2, :] + nxt * w_ref[2:3, :]


def _route(h2, rwt_ref, rb_ref, carry_ref, idx_ref, rank_ref, gate_ref, cnt_ref):
    logits = lax.dot_general(rwt_ref[...], h2, (((1,), (1,)), ((), ())), preferred_element_type=F32,
                             precision=HIGHEST) + rb_ref[...]
    eid = lax.broadcasted_iota(jnp.int32, (N_EXPERTS, TM), 0)
    work = logits
    sel = jnp.zeros((N_EXPERTS, TM), F32)
    vals, ids, hots = [], [], []
    for _ in range(TOP_K):
        m = jnp.max(work, axis=0, keepdims=True)
        ik = jnp.min(jnp.where(work == m, eid, N_EXPERTS), axis=0, keepdims=True)
        hot = eid == ik
        work = jnp.where(hot, NEG, work)
        sel = sel + hot.astype(F32)
        vals.append(m)
        ids.append(ik)
        hots.append(hot)
    es = [jnp.exp(v - vals[0]) for v in vals]
    den = es[0] + es[1] + es[2] + es[3]
    earlier = (lax.broadcasted_iota(jnp.int32, (TM, TM), 0) < lax.broadcasted_iota(jnp.int32, (TM, TM), 1))
    rank_all = jnp.dot(sel.astype(BF16), earlier.astype(BF16), preferred_element_type=F32) + carry_ref[:, 0:1]
    slot = lax.broadcasted_iota(jnp.int32, (8, TM), 0)
    idx_o = jnp.zeros((8, TM), jnp.int32)
    rank_o = jnp.zeros((8, TM), jnp.int32)
    gate_o = jnp.zeros((8, TM), F32)
    for k in range(TOP_K):
        rk = jnp.sum(jnp.where(hots[k], rank_all, 0.0), axis=0, keepdims=True)
        idx_o = jnp.where(slot == k, ids[k], idx_o)
        rank_o = jnp.where(slot == k, rk.astype(jnp.int32), rank_o)
        gate_o = jnp.where(slot == k, es[k] / den, gate_o)
    idx_ref[...] = idx_o
    rank_ref[...] = rank_o
    gate_ref[...] = gate_o
    total = carry_ref[...] + jnp.sum(sel, axis=1, keepdims=True)
    carry_ref[...] = total
    cnt_ref[...] = total


def _tile_maps(n_ctx_tiles, tiles_per_seq):
    def grp(i):
        return jnp.where(i < n_ctx_tiles, 0, 1 + (i - n_ctx_tiles) // tiles_per_seq)

    tok = lambda i: (i, 0)
    const = lambda i: (0, 0)
    mod = lambda i: (grp(i), 0, 0)
    return tok, const, mod


def _route_specs(const):
    return [
        pl.BlockSpec((1, D), const),
        pl.BlockSpec((N_EXPERTS, D), const),
        pl.BlockSpec((N_EXPERTS, 1), const),
    ]


def _route_outs(n_tok, tok, const):
    specs = [
        pl.BlockSpec((TM, D), tok),
        pl.BlockSpec((TM, D), tok),
        pl.BlockSpec((8, TM), lambda i: (0, i)),
        pl.BlockSpec((8, TM), lambda i: (0, i)),
        pl.BlockSpec((8, TM), lambda i: (0, i)),
        pl.BlockSpec((N_EXPERTS, LANES), const),
    ]
    shapes = [
        jax.ShapeDtypeStruct((n_tok, D), F32),
        jax.ShapeDtypeStruct((n_tok, D), F32),
        jax.ShapeDtypeStruct((8, n_tok), jnp.int32),
        jax.ShapeDtypeStruct((8, n_tok), jnp.int32),
        jax.ShapeDtypeStruct((8, n_tok), F32),
        jax.ShapeDtypeStruct((N_EXPERTS, LANES), F32),
    ]
    return specs, shapes


def _moe_front(x1, mod_ref, n2_ref, rw_ref, rb_ref, carry_ref, x1_ref, h2_ref, idx_ref, rank_ref, gate_ref,
               cnt_ref):
    @pl.when(pl.program_id(0) == 0)
    def _():
        carry_ref[...] = jnp.zeros_like(carry_ref)

    x1_ref[...] = x1
    h2 = _prenorm(x1, n2_ref[...], mod_ref[4:5, :], mod_ref[3:4, :])
    h2_ref[...] = h2
    _route(h2, rw_ref, rb_ref, carry_ref, idx_ref, rank_ref, gate_ref, cnt_ref)


def _conv_body(n_ctx_tiles, x_ref, mod_ref, n1_ref, win_ref, cw_ref, wout_ref, n2_ref, rw_ref, rb_ref,
               x1_ref, h2_ref, idx_ref, rank_ref, gate_ref, cnt_ref, carry_ref):
    i = pl.program_id(0)
    row_len = jnp.where(i < n_ctx_tiles, CTX_ROW, GRID_W)
    x = x_ref[...]
    h = _prenorm(x, n1_ref[...], mod_ref[1:2, :], mod_ref[0:1, :])
    proj = _bdot(h, win_ref[...])
    b, cg, u = proj[:, :D], proj[:, D:2 * D], proj[:, 2 * D:]
    mix = _bdot(b * _short_conv(cg * u, cw_ref, row_len), wout_ref[...])
    x1 = x + mod_ref[2:3, :] * mix
    _moe_front(x1, mod_ref, n2_ref, rw_ref, rb_ref, carry_ref, x1_ref, h2_ref, idx_ref, rank_ref, gate_ref,
               cnt_ref)


def _conv_layer(x, mod, n1, win, cw, wout, n2, rw, rb, n_ctx_tiles, tiles_per_seq):
    n_tok = x.shape[0]
    tok, const, modm = _tile_maps(n_ctx_tiles, tiles_per_seq)
    out_specs, out_shapes = _route_outs(n_tok, tok, const)
    return pl.pallas_call(
        functools.partial(_conv_body, n_ctx_tiles),
        grid=(n_tok // TM,),
        in_specs=[
            pl.BlockSpec((TM, D), tok),
            pl.BlockSpec((None, 6, D), modm),
            pl.BlockSpec((1, D), const),
            pl.BlockSpec((D, 3 * D), const),
            pl.BlockSpec((3, D), const),
            pl.BlockSpec((D, D), const),
        ] + _route_specs(const),
        out_specs=out_specs,
        out_shape=out_shapes,
        scratch_shapes=[pltpu.VMEM((N_EXPERTS, LANES), F32)],
        compiler_params=_params(("arbitrary",)),
        name="conv_layer",
    )(x, mod, n1, win, cw, wout, n2, rw, rb)


def _scatter_body(dest_ref, pend_ref, h_ref, xs_ref, zbuf, sem, zsem):
    i = pl.program_id(0)
    n_tok = pl.num_programs(0) * TM

    @pl.when(i == 0)
    def _():
        zbuf[...] = jnp.zeros_like(zbuf)

        def zero_block(j):
            return pltpu.make_async_copy(zbuf, xs_ref.at[pl.ds(pl.multiple_of(j * BLK, BLK), BLK)], zsem)

        def start(j, carry):
            zero_block(j).start()
            return carry

        def wait(j, carry):
            zero_block(j).wait()
            return carry

        n_used = pend_ref[N_EXPERTS - 1] // BLK
        lax.fori_loop(n_used, xs_ref.shape[0] // BLK, start, 0)
        lax.fori_loop(n_used, xs_ref.shape[0] // BLK, wait, 0)
        for phase in range(2):
            for e in range(N_EXPERTS):
                end = pend_ref[e]
                begin = pend_ref[e - 1] if e else 0

                @pl.when(end > begin)
                def _():
                    tail = pl.ds(pl.multiple_of(end - BLK, BLK), BLK)
                    cp = pltpu.make_async_copy(zbuf, xs_ref.at[tail], zsem)
                    if phase == 0:
                        cp.start()
                    else:
                        cp.wait()

    base = i * TM

    def row_copy(t, k):
        return pltpu.make_async_copy(h_ref.at[pl.ds(t, 1)], xs_ref.at[pl.ds(dest_ref[k * n_tok + base + t], 1)],
                                     sem)

    def issue(t, carry):
        for k in range(TOP_K):
            row_copy(t, k).start()
        return carry

    lax.fori_loop(0, TM, issue, 0)
    for _ in range(TOP_K):
        pltpu.make_async_copy(h_ref, xs_ref.at[pl.ds(0, TM)], sem).wait()


def _scatter(dest, pend, h2, n_slots):
    n_tok = h2.shape[0]
    return pl.pallas_call(
        _scatter_body,
        grid_spec=pltpu.PrefetchScalarGridSpec(
            num_scalar_prefetch=2,
            grid=(n_tok // TM,),
            in_specs=[pl.BlockSpec((TM, D), lambda i, *_: (i, 0))],
            out_specs=pl.BlockSpec(memory_space=pl.ANY),
            scratch_shapes=[pltpu.VMEM((BLK, D), F32), pltpu.SemaphoreType.DMA, pltpu.SemaphoreType.DMA],
        ),
        out_shape=jax.ShapeDtypeStruct((n_slots, D), F32),
        compiler_params=_params(("arbitrary",)),
        name="moe_scatter",
    )(dest, pend, h2)


W_ROWS = 128


def _expert_body(be_ref, nu_ref, x_ref, wgu_ref, bgu_ref, wd_ref, bd_ref, y_ref, wgu_bf, wd_bf):
    j = pl.program_id(0)
    n_used = nu_ref[0]
    jj = jnp.minimum(j, n_used - 1)
    e = be_ref[jj]
    e_prev = be_ref[jnp.maximum(jj - 1, 0)]

    @pl.when((j == 0) | (e != e_prev))
    def _():
        def cast(r, carry):
            rows = pl.ds(pl.multiple_of(r * W_ROWS, W_ROWS), W_ROWS)
            wgu_bf[rows, :] = wgu_ref[rows, :].astype(BF16)
            wd_bf[rows, :] = wd_ref[rows, :].astype(BF16)
            return carry

        lax.fori_loop(0, D // W_ROWS, cast, 0)

    @pl.when(j < n_used)
    def _():
        hh = jnp.dot(x_ref[...].astype(BF16), wgu_bf[...], preferred_element_type=F32) + bgu_ref[...]
        hg = jnp.minimum(hh[:, :D_FF], SWIGLU_LIMIT)
        hl = jnp.clip(hh[:, D_FF:], -SWIGLU_LIMIT, SWIGLU_LIMIT)
        act = hg * jax.nn.sigmoid(SWIGLU_ALPHA * hg) * (hl + 1.0)
        y_ref[...] = jnp.dot(act.astype(BF16), wd_bf[...], preferred_element_type=F32) + bd_ref[...]

    @pl.when(j >= n_used)
    def _():
        y_ref[...] = jnp.zeros_like(y_ref)


def _experts(block_e, n_used, xs, layer, wgu, bgu, wd, bd):
    n_slots = xs.shape[0]
    n_layers = wgu.shape[0]

    def blk(j, be, nu):
        return (jnp.minimum(j, nu[0] - 1), 0)

    def wmap(j, be, nu):
        return (layer, be[jnp.minimum(j, nu[0] - 1)], 0, 0)

    return pl.pallas_call(
        _expert_body,
        grid_spec=pltpu.PrefetchScalarGridSpec(
            num_scalar_prefetch=2,
            grid=(n_slots // BLK,),
            in_specs=[
                pl.BlockSpec((BLK, D), blk),
                pl.BlockSpec((None, None, D, 2 * D_FF), wmap),
                pl.BlockSpec((None, None, 1, 2 * D_FF), wmap),
                pl.BlockSpec((None, None, D_FF, D), wmap),
                pl.BlockSpec((None, None, 1, D), wmap),
            ],
            out_specs=pl.BlockSpec((BLK, D), lambda j, be, nu: (j, 0)),
            scratch_shapes=[pltpu.VMEM((D, 2 * D_FF), BF16), pltpu.VMEM((D_FF, D), BF16)],
        ),
        out_shape=jax.ShapeDtypeStruct((n_slots, D), F32),
        compiler_params=_params(("arbitrary",)),
        name="moe_experts",
    )(block_e, n_used, xs, wgu, bgu.reshape(n_layers, N_EXPERTS, 1, 2 * D_FF), wd,
      bd.reshape(n_layers, N_EXPERTS, 1, D))


def _combine_body(final, dest_ref, x_ref, gate_ref, mod_ref, fnw_ref, y_ref, o_ref, buf, sem):
    i = pl.program_id(0)
    n_tok = pl.num_programs(0) * TM
    base = i * TM

    def row_copy(t, k):
        return pltpu.make_async_copy(y_ref.at[pl.ds(dest_ref[k * n_tok + base + t], 1)],
                                     buf.at[k, pl.ds(t, 1)], sem)

    def issue(t, carry):
        for k in range(TOP_K):
            row_copy(t, k).start()
        return carry

    lax.fori_loop(0, TM, issue, 0)
    for k in range(TOP_K):
        pltpu.make_async_copy(y_ref.at[pl.ds(0, TM)], buf.at[k], sem).wait()
    gate = gate_ref[...].T
    acc = gate[:, 0:1] * buf[0]
    for k in range(1, TOP_K):
        acc = acc + gate[:, k:k + 1] * buf[k]
    x2 = x_ref[...] + mod_ref[5:6, :] * acc
    if final:
        ms = jnp.mean(x2 * x2, axis=-1, keepdims=True)
        x2 = x2 * lax.rsqrt(ms + EPS) * fnw_ref[...]
    o_ref[...] = x2


def _combine(dest, x1, gate, mod, fnw, y_slots, n_ctx_tiles, tiles_per_seq, final):
    n_tok = x1.shape[0]

    def grp(i):
        return jnp.where(i < n_ctx_tiles, 0, 1 + (i - n_ctx_tiles) // tiles_per_seq)

    return pl.pallas_call(
        functools.partial(_combine_body, final),
        grid_spec=pltpu.PrefetchScalarGridSpec(
            num_scalar_prefetch=1,
            grid=(n_tok // TM,),
            in_specs=[
                pl.BlockSpec((TM, D), lambda i, d: (i, 0)),
                pl.BlockSpec((8, TM), lambda i, d: (0, i)),
                pl.BlockSpec((None, 6, D), lambda i, d: (grp(i), 0, 0)),
                pl.BlockSpec((1, D), lambda i, d: (0, 0)),
                pl.BlockSpec(memory_space=pl.ANY),
            ],
            out_specs=pl.BlockSpec((TM, D), lambda i, d: (i, 0)),
            scratch_shapes=[pltpu.VMEM((TOP_K, TM, D), F32), pltpu.SemaphoreType.DMA],
        ),
        out_shape=jax.ShapeDtypeStruct((n_tok, D), F32),
        compiler_params=_params(("arbitrary",)),
        name="moe_combine",
    )(dest, x1, gate, mod, fnw, y_slots)


def _moe(x1, h2, idx, rank, gate, counts, mod, fnw, layer, wgu, bgu, wd, bd, n_ctx_tiles, tiles_per_seq, final):
    n_tok = x1.shape[0]
    n_slots = n_tok * TOP_K + N_EXPERTS * BLK
    cnt = counts[:, 0].astype(jnp.int32)
    padded = (cnt + BLK - 1) // BLK * BLK
    pend = jnp.cumsum(padded)
    pstart = pend - padded
    dest = (pstart[idx[:TOP_K]] + rank[:TOP_K]).reshape(-1).astype(jnp.int32)
    n_blocks = n_slots // BLK
    block_start = jnp.arange(n_blocks, dtype=jnp.int32)[:, None] * BLK
    block_e = jnp.minimum(jnp.sum(pend[None, :] <= block_start, axis=1), N_EXPERTS - 1).astype(jnp.int32)
    n_used = (pend[-1:] // BLK).astype(jnp.int32)
    xs = _scatter(dest, pend.astype(jnp.int32), h2, n_slots)
    ys = _experts(block_e, n_used, xs, layer, wgu, bgu, wd, bd)
    return _combine(dest, x1, gate, mod, fnw, ys, n_ctx_tiles, tiles_per_seq, final)


def _dn_in_body(n_ctx_tiles, x_ref, mod_ref, n1_ref, w_ref, wba_ref, cw_ref, alog_ref, dtb_ref,
                q_ref, k_ref, v_ref, z_ref, bg_ref):
    i = pl.program_id(0)
    row_len = jnp.where(i < n_ctx_tiles, CTX_ROW, GRID_W)
    h = _prenorm(x_ref[...], n1_ref[...], mod_ref[1:2, :], mod_ref[0:1, :]).astype(BF16)
    proj = jnp.dot(h, w_ref[...], preferred_element_type=F32)
    z_ref[...] = proj[:, CONV_CH:]
    qkv = _silu(_short_conv(proj[:, :CONV_CH], cw_ref, row_len))
    for hd in range(2 * HK):
        s = qkv[:, hd * DK:(hd + 1) * DK]
        n = s * lax.rsqrt(jnp.sum(s * s, axis=-1, keepdims=True) + EPS)
        if hd < HK:
            q_ref[:, hd * DK:(hd + 1) * DK] = n * (DK ** -0.5)
        else:
            k_ref[:, (hd - HK) * DK:(hd - HK + 1) * DK] = n
    v_ref[...] = qkv[:, 2 * QK_DIM:]
    ba = jnp.dot(h, wba_ref[...], preferred_element_type=F32)
    beta = jax.nn.sigmoid(ba)
    a = ba + dtb_ref[...]
    softplus = jnp.maximum(a, 0.0) + jnp.log(1.0 + jnp.exp(-jnp.abs(a)))
    g = -jnp.exp(alog_ref[...]) * softplus
    lane = lax.broadcasted_iota(jnp.int32, (TM, LANES), 1)
    bg_ref[...] = jnp.where(lane < G0, beta, g)


def _dn_in(x, mod, n1, w, wba, cw, alog, dtb, n_ctx_tiles, tiles_per_seq):
    n_tok = x.shape[0]
    tok, const, modm = _tile_maps(n_ctx_tiles, tiles_per_seq)
    return pl.pallas_call(
        functools.partial(_dn_in_body, n_ctx_tiles),
        grid=(n_tok // TM,),
        in_specs=[
            pl.BlockSpec((TM, D), tok),
            pl.BlockSpec((None, 6, D), modm),
            pl.BlockSpec((1, D), const),
            pl.BlockSpec((D, CONV_CH + V_DIM), const),
            pl.BlockSpec((D, LANES), const),
            pl.BlockSpec((3, CONV_CH), const),
            pl.BlockSpec((1, LANES), const),
            pl.BlockSpec((1, LANES), const),
        ],
        out_specs=[
            pl.BlockSpec((TM, QK_DIM), tok),
            pl.BlockSpec((TM, QK_DIM), tok),
            pl.BlockSpec((TM, V_DIM), tok),
            pl.BlockSpec((TM, V_DIM), tok),
            pl.BlockSpec((TM, LANES), tok),
        ],
        out_shape=[
            jax.ShapeDtypeStruct((n_tok, QK_DIM), F32),
            jax.ShapeDtypeStruct((n_tok, QK_DIM), F32),
            jax.ShapeDtypeStruct((n_tok, V_DIM), F32),
            jax.ShapeDtypeStruct((n_tok, V_DIM), F32),
            jax.ShapeDtypeStruct((n_tok, LANES), F32),
        ],
        compiler_params=_params(("arbitrary",)),
        name="dn_in",
    )(x, mod, n1, w, wba, cw, alog, dtb)


def _hdot(a, b):
    return jnp.dot(a, b, preferred_element_type=F32, precision=HIGHEST)


def _nt_dot(a, b):
    return lax.dot_general(a, b, (((1,), (1,)), ((), ())), preferred_element_type=F32)


def _diag2(a, b):
    a, b = a.astype(BF16), b.astype(BF16)
    z = jnp.zeros_like(a)
    return jnp.concatenate([jnp.concatenate([a, z], axis=1), jnp.concatenate([z, b], axis=1)], axis=0)


def _delta_dir(bwd, q_ref, k_ref, v_ref, bg_ref, o_ref, state):
    r2 = lax.broadcasted_iota(jnp.int32, (CHUNK, LANES), 0)
    lane = lax.broadcasted_iota(jnp.int32, (CHUNK, LANES), 1)
    c2 = lane & (CHUNK - 1)
    left = lane < CHUNK
    causal2 = (r2 <= c2) if bwd else (r2 >= c2)
    strict2 = (r2 < c2) if bwd else (r2 > c2)
    eye2 = (r2 == c2).astype(F32)
    lvl = r2 ^ c2
    b0 = HV if bwd else 0
    g0 = 2 * HV + b0
    bg = bg_ref[...]
    gc = _hdot(causal2[:, :CHUNK].astype(F32), bg)
    gct = gc.T
    gtot = jnp.sum(bg, axis=0, keepdims=True)
    egc = jnp.exp(gc)
    eend = jnp.exp(gtot - gc)
    etot = jnp.exp(gtot)

    def pair_cols(a, j):
        return jnp.where(left, a[:, j:j + 1], a[:, j + 1:j + 2])

    def bdiag(x):
        z = jnp.zeros_like(x)
        return jnp.concatenate([jnp.where(left, x, z), jnp.where(left, z, x)], axis=0).astype(BF16)

    ns, attns = [], []
    for p in range(HK):
        j = g0 + PAIR * p
        k16 = k_ref[:, p * DK:(p + 1) * DK].astype(BF16)
        kcat = jnp.concatenate([k16, k16], axis=0)
        kk2 = _nt_dot(k16, kcat)
        qk2 = _nt_dot(q_ref[:, p * DK:(p + 1) * DK].astype(BF16), kcat)
        grow2 = jnp.concatenate([gct[j:j + 1, :], gct[j + 1:j + 2, :]], axis=1)
        decay2 = jnp.exp(jnp.where(causal2, pair_cols(gc, j) - grow2, NEG))
        ns.append(jnp.where(strict2, kk2 * decay2 * pair_cols(bg, b0 + PAIR * p), 0.0))
        attns.append(qk2 * decay2)

    def mdot(a, b):
        return jnp.dot(a.astype(BF16), bdiag(b), preferred_element_type=F32)

    n4 = [jnp.where(lvl < 4, n, 0.0) for n in ns]
    sq = [mdot(a, a) for a in n4]
    ts = [eye2 - a for a in n4]
    ts = [t + mdot(t, s) for t, s in zip(ts, sq)]
    for bit in range(2, 6):
        ys = [mdot(t, jnp.where((lvl >> bit) == 1, n, 0.0)) for t, n in zip(ts, ns)]
        ts = [t - mdot(y, t) for t, y in zip(ts, ys)]

    sols, ss = [], []
    for p in range(HK):
        kh = k_ref[:, p * DK:(p + 1) * DK]
        rhs = []
        for e in range(PAIR):
            h = PAIR * p + e
            beta = bg[:, b0 + h:b0 + h + 1]
            rhs.append(jnp.concatenate(
                [v_ref[:, h * DV:(h + 1) * DV] * beta, kh * (beta * egc[:, g0 + h:g0 + h + 1])], axis=1))
        sols.append(jnp.dot(ts[p].astype(BF16), _diag2(rhs[0], rhs[1]), preferred_element_type=F32))
        ss.append((state[bwd, PAIR * p], state[bwd, PAIR * p + 1]))

    outs = []
    for p in range(HK):
        h0 = PAIR * p
        qh = q_ref[:, p * DK:(p + 1) * DK]
        sol = sols[p]
        w2 = jnp.concatenate([sol[:, DV:2 * DV], sol[:, 3 * DV:]], axis=1)
        qg2 = jnp.concatenate([qh * egc[:, g0 + h0:g0 + h0 + 1], qh * egc[:, g0 + h0 + 1:g0 + h0 + 2]], axis=1)
        lhs = jnp.concatenate([w2, qg2], axis=0).astype(BF16)
        outs.append(jnp.dot(lhs, _diag2(ss[p][0], ss[p][1]), preferred_element_type=F32))

    for p in range(HK):
        h0 = PAIR * p
        kh = k_ref[:, p * DK:(p + 1) * DK]
        sol, out = sols[p], outs[p]
        vn0 = sol[:, :DV] - out[:CHUNK, :DV]
        vn1 = sol[:, 2 * DV:3 * DV] - out[:CHUNK, DV:]
        o_ref[:, h0 * DV:(h0 + PAIR) * DV] = out[CHUNK:, :] + jnp.dot(
            attns[p].astype(BF16), _diag2(vn0, vn1), preferred_element_type=F32)
        kend2 = jnp.concatenate([kh * eend[:, g0 + h0:g0 + h0 + 1], kh * eend[:, g0 + h0 + 1:g0 + h0 + 2]], axis=1)
        kv = lax.dot_general(kend2.astype(BF16), jnp.concatenate([vn0, vn1], axis=1).astype(BF16),
                             (((0,), (0,)), ((), ())), preferred_element_type=F32)
        state[bwd, h0] = ss[p][0] * etot[:, g0 + h0:g0 + h0 + 1] + kv[:DK, :DV]
        state[bwd, h0 + 1] = ss[p][1] * etot[:, g0 + h0 + 1:g0 + h0 + 2] + kv[DK:, DV:]


FIRST, LAST, DEC = 1, 2, 4


def _delta_body(fb_ref, bb_ref, si_ref, oi_ref, fl_ref, qf_ref, kf_ref, vf_ref, bgf_ref, qb_ref, kb_ref, vb_ref,
                bgb_ref, s0_ref, of_ref, ob_ref, sout_ref, state):
    flags = fl_ref[pl.program_id(0)]

    @pl.when((flags & FIRST) != 0)
    def _():
        state[...] = jnp.where((flags & DEC) != 0, s0_ref[...], 0.0)

    _delta_dir(0, qf_ref, kf_ref, vf_ref, bgf_ref, of_ref, state)
    _delta_dir(1, qb_ref, kb_ref, vb_ref, bgb_ref, ob_ref, state)

    @pl.when((flags & (LAST | DEC)) == LAST)
    def _():
        sout_ref[...] = state[...]


def _delta(q, k, v, bg, state_delta, n_ctx, ctx_chunks, n_dec, dec_chunks):
    n_tok = q.shape[0]
    fb, bb, si, oi, fl = [], [], [], [], []
    for dec, n_seq, n, base in ((0, n_ctx, ctx_chunks, 0), (1, n_dec, dec_chunks, n_ctx * ctx_chunks)):
        for b in range(n_seq):
            for c in range(n):
                fb.append(base + b * n + c)
                bb.append(base + b * n + n - 1 - c)
                si.append(b if dec else 0)
                oi.append(n_ctx - 1 if dec else b)
                fl.append((FIRST if c == 0 else 0) | (LAST if c == n - 1 else 0) | (DEC if dec else 0))
    tables = [jnp.asarray(t, jnp.int32) for t in (fb, bb, si, oi, fl)]
    fwd = lambda s, fb, bb, si, oi, fl: (fb[s], 0)
    bwd = lambda s, fb, bb, si, oi, fl: (bb[s], 0)
    tok_specs = lambda m: [pl.BlockSpec((CHUNK, QK_DIM), m), pl.BlockSpec((CHUNK, QK_DIM), m),
                           pl.BlockSpec((CHUNK, V_DIM), m), pl.BlockSpec((CHUNK, LANES), m)]
    return pl.pallas_call(
        _delta_body,
        grid_spec=pltpu.PrefetchScalarGridSpec(
            num_scalar_prefetch=5,
            grid=(len(fb),),
            in_specs=tok_specs(fwd) + tok_specs(bwd) + [
                pl.BlockSpec((None, None, 2, HV, DK, DV), lambda s, fb, bb, si, oi, fl: (si[s], 0, 0, 0, 0, 0)),
            ],
            out_specs=[
                pl.BlockSpec((CHUNK, V_DIM), fwd),
                pl.BlockSpec((CHUNK, V_DIM), bwd),
                pl.BlockSpec((None, 2, HV, DK, DV), lambda s, fb, bb, si, oi, fl: (oi[s], 0, 0, 0, 0)),
            ],
            scratch_shapes=[pltpu.VMEM((2, HV, DK, DV), F32)],
        ),
        out_shape=[
            jax.ShapeDtypeStruct((n_tok, V_DIM), F32),
            jax.ShapeDtypeStruct((n_tok, V_DIM), F32),
            jax.ShapeDtypeStruct((n_ctx, 2, HV, DK, DV), F32),
        ],
        compiler_params=_params(("arbitrary",)),
        name="delta_rule",
    )(*tables, q, k, v, bg, q, k, v, bg, state_delta)


def _dn_out_body(of_ref, ob_ref, z_ref, x_ref, mod_ref, nw_ref, wout_ref, n2_ref, rw_ref, rb_ref,
                 x1_ref, h2_ref, idx_ref, rank_ref, gate_ref, cnt_ref, carry_ref):
    o = of_ref[...] + ob_ref[...]
    z = z_ref[...]
    parts = []
    for h in range(HV):
        oh = o[:, h * DV:(h + 1) * DV]
        nh = oh * lax.rsqrt(jnp.mean(oh * oh, axis=-1, keepdims=True) + EPS) * nw_ref[...]
        parts.append((nh * _silu(z[:, h * DV:(h + 1) * DV])).astype(BF16))
    mix = jnp.dot(jnp.concatenate(parts, axis=1), wout_ref[...], preferred_element_type=F32)
    x1 = x_ref[...] + mod_ref[2:3, :] * mix
    _moe_front(x1, mod_ref, n2_ref, rw_ref, rb_ref, carry_ref, x1_ref, h2_ref, idx_ref, rank_ref, gate_ref,
               cnt_ref)


def _dn_out(o_f, o_b, z, x, mod, nw, wout, n2, rw, rb, n_ctx_tiles, tiles_per_seq):
    n_tok = x.shape[0]
    tok, const, modm = _tile_maps(n_ctx_tiles, tiles_per_seq)
    out_specs, out_shapes = _route_outs(n_tok, tok, const)
    return pl.pallas_call(
        _dn_out_body,
        grid=(n_tok // TM,),
        in_specs=[
            pl.BlockSpec((TM, V_DIM), tok),
            pl.BlockSpec((TM, V_DIM), tok),
            pl.BlockSpec((TM, V_DIM), tok),
            pl.BlockSpec((TM, D), tok),
            pl.BlockSpec((None, 6, D), modm),
            pl.BlockSpec((1, DV), const),
            pl.BlockSpec((V_DIM, D), const),
        ] + _route_specs(const),
        out_specs=out_specs,
        out_shape=out_shapes,
        scratch_shapes=[pltpu.VMEM((N_EXPERTS, LANES), F32)],
        compiler_params=_params(("arbitrary",)),
        name="dn_out",
    )(o_f, o_b, z, x, mod, nw, wout, n2, rw, rb)


def kernel(x_prompt, x_sample, state_delta, c, c_ctx, ada_w, ada_b, norm1_w, norm2_w, conv_in_w, conv_w,
           conv_out_w, dn_in_w, dn_conv_w, dn_a_log, dn_dt_bias, dn_norm_w, dn_out_w, router_w, router_b,
           exp_gu_w, exp_gu_b, exp_down_w, exp_down_b, final_norm_w):
    n_ctx, ctx_len, _ = x_prompt.shape
    n_dec, dec_len, _ = x_sample.shape
    assert ctx_len == CTX_ROW and dec_len % TM == 0 and n_dec + 1 <= N_COND
    n_ctx_tok = n_ctx * ctx_len
    n_ctx_tiles = n_ctx_tok // TM
    tiles_per_seq = dec_len // TM

    x = jnp.concatenate([x_prompt.reshape(n_ctx_tok, D), x_sample.reshape(n_dec * dec_len, D)], axis=0)
    cond = jnp.zeros((N_COND, D), F32).at[0].set(c_ctx).at[1:1 + n_dec].set(c)
    mod = _ada(cond, ada_w, ada_b)

    def router(l):
        return norm2_w[l][None, :], router_w[l].T, router_b[l][:, None]

    def experts(l):
        return l, exp_gu_w, exp_gu_b, exp_down_w, exp_down_b

    front = _conv_layer(x, mod[0], norm1_w[0][None, :], conv_in_w[0].astype(BF16), conv_w[0],
                        conv_out_w[0].astype(BF16), *router(0), n_ctx_tiles, tiles_per_seq)
    x = _moe(*front, mod[0], final_norm_w[None, :], *experts(0), n_ctx_tiles, tiles_per_seq, False)

    w_in = dn_in_w[0]
    w_ba = jnp.zeros((D, LANES), F32).at[:, :4 * HV].set(w_in[:, CONV_CH + V_DIM:]).astype(BF16)
    lane_pad = lambda a: jnp.zeros((1, LANES), F32).at[0, 2 * HV:4 * HV].set(a.reshape(-1))
    q, k, v, z, bg = _dn_in(x, mod[1], norm1_w[1][None, :], w_in[:, :CONV_CH + V_DIM].astype(BF16), w_ba,
                            dn_conv_w[0], lane_pad(dn_a_log[0]), lane_pad(dn_dt_bias[0]), n_ctx_tiles,
                            tiles_per_seq)
    o_f, o_b, s_ctx = _delta(q, k, v, bg, state_delta, n_ctx, ctx_len // CHUNK, n_dec, dec_len // CHUNK)
    front = _dn_out(o_f, o_b, z, x, mod[1], dn_norm_w[0][None, :], dn_out_w[0].astype(BF16), *router(1),
                    n_ctx_tiles, tiles_per_seq)
    y = _moe(*front, mod[1], final_norm_w[None, :], *experts(1), n_ctx_tiles, tiles_per_seq, True)

    y_prompt = y[:n_ctx_tok].reshape(n_ctx, ctx_len, D)
    y_sample = y[n_ctx_tok:].reshape(n_dec, dec_len, D)
    return y_prompt, y_sample, s_ctx[:, None]
```

```python
import functools

import jax
import jax.numpy as jnp
from jax import lax
from jax.experimental import pallas as pl
from jax.experimental.pallas import tpu as pltpu

F32 = jnp.float32
BF16 = jnp.bfloat16

D = 1024
TM = 256
CTX_ROW = 256
GRID_W = 64
HK, HV, DK, DV = 8, 16, 128, 128
QK_DIM = HK * DK
V_DIM = HV * DV
CONV_CH = 2 * QK_DIM + V_DIM
CHUNK = 64
N_EXPERTS = 32
TOP_K = 4
D_FF = 1024
SWIGLU_ALPHA = 1.702
SWIGLU_LIMIT = 7.0
EPS = 1e-6
LANES = 128
N_COND = 8
BLK = 512
NEG = -3.0e38
VMEM_LIMIT = 56 * 1024 * 1024
HIGHEST = lax.Precision.HIGHEST
G0 = 2 * HV
PAIR = HV // HK


def _silu(x):
    return x * jax.nn.sigmoid(x)


def _bdot(a, b):
    return jnp.dot(a.astype(BF16), b.astype(BF16), preferred_element_type=F32)


def _params(sem):
    return pltpu.CompilerParams(dimension_semantics=sem, vmem_limit_bytes=VMEM_LIMIT)


def _ada_body(c_ref, w_ref, b_ref, o_ref):
    o_ref[...] = _bdot(_silu(c_ref[...]), w_ref[...]) + b_ref[...]


def _ada(cond, ada_w, ada_b):
    n_layers = ada_w.shape[0]
    tn = 1024
    out = pl.pallas_call(
        _ada_body,
        grid=(n_layers, 6 * D // tn),
        in_specs=[
            pl.BlockSpec((N_COND, D), lambda l, j: (0, 0)),
            pl.BlockSpec((None, D, tn), lambda l, j: (l, 0, j)),
            pl.BlockSpec((None, 1, tn), lambda l, j: (l, 0, j)),
        ],
        out_specs=pl.BlockSpec((None, N_COND, tn), lambda l, j: (l, 0, j)),
        out_shape=jax.ShapeDtypeStruct((n_layers, N_COND, 6 * D), F32),
        compiler_params=_params(("arbitrary", "arbitrary")),
        name="ada",
    )(cond, ada_w, ada_b.reshape(n_layers, 1, 6 * D))
    return out.reshape(n_layers, N_COND, 6, D)


def _prenorm(x, nw, sc, sh):
    ms = jnp.mean(x * x, axis=-1, keepdims=True)
    return (x * lax.rsqrt(ms + EPS) * nw) * (1.0 + sc) + sh


def _short_conv(v, w_ref, row_len):
    n = v.shape[0]
    pos = lax.broadcasted_iota(jnp.int32, (n, 1), 0) & (row_len - 1)
    prev = jnp.where(pos == 0, 0.0, pltpu.roll(v, 1, 0))
    nxt = jnp.where(pos == row_len - 1, 0.0, pltpu.roll(v, n - 1, 0))
    return prev * w_ref[0:1, :] + v * w_ref[1:2, :] + nxt * w_ref[2:3, :]


def _route(h2, rwt_ref, rb_ref, carry_ref, idx_ref, rank_ref, gate_ref, cnt_ref):
    logits = lax.dot_general(rwt_ref[...], h2, (((1,), (1,)), ((), ())), preferred_element_type=F32,
                             precision=HIGHEST) + rb_ref[...]
    eid = lax.broadcasted_iota(jnp.int32, (N_EXPERTS, TM), 0)
    work = logits
    sel = jnp.zeros((N_EXPERTS, TM), F32)
    vals, ids, hots = [], [], []
    for _ in range(TOP_K):
        m = jnp.max(work, axis=0, keepdims=True)
        ik = jnp.min(jnp.where(work == m, eid, N_EXPERTS), axis=0, keepdims=True)
        hot = eid == ik
        work = jnp.where(hot, NEG, work)
        sel = sel + hot.astype(F32)
        vals.append(m)
        ids.append(ik)
        hots.append(hot)
    es = [jnp.exp(v - vals[0]) for v in vals]
    den = es[0] + es[1] + es[2] + es[3]
    earlier = (lax.broadcasted_iota(jnp.int32, (TM, TM), 0) < lax.broadcasted_iota(jnp.int32, (TM, TM), 1))
    rank_all = jnp.dot(sel.astype(BF16), earlier.astype(BF16), preferred_element_type=F32) + carry_ref[:, 0:1]
    slot = lax.broadcasted_iota(jnp.int32, (8, TM), 0)
    idx_o = jnp.zeros((8, TM), jnp.int32)
    rank_o = jnp.zeros((8, TM), jnp.int32)
    gate_o = jnp.zeros((8, TM), F32)
    for k in range(TOP_K):
        rk = jnp.sum(jnp.where(hots[k], rank_all, 0.0), axis=0, keepdims=True)
        idx_o = jnp.where(slot == k, ids[k], idx_o)
        rank_o = jnp.where(slot == k, rk.astype(jnp.int32), rank_o)
        gate_o = jnp.where(slot == k, es[k] / den, gate_o)
    idx_ref[...] = idx_o
    rank_ref[...] = rank_o
    gate_ref[...] = gate_o
    total = carry_ref[...] + jnp.sum(sel, axis=1, keepdims=True)
    carry_ref[...] = total
    cnt_ref[...] = total


def _tile_maps(n_ctx_tiles, tiles_per_seq):
    def grp(i):
        return jnp.where(i < n_ctx_tiles, 0, 1 + (i - n_ctx_tiles) // tiles_per_seq)

    tok = lambda i: (i, 0)
    const = lambda i: (0, 0)
    mod = lambda i: (grp(i), 0, 0)
    return tok, const, mod


def _route_specs(const):
    return [
        pl.BlockSpec((1, D), const),
        pl.BlockSpec((N_EXPERTS, D), const),
        pl.BlockSpec((N_EXPERTS, 1), const),
    ]


def _route_outs(n_tok, tok, const):
    specs = [
        pl.BlockSpec((TM, D), tok),
        pl.BlockSpec((TM, D), tok),
        pl.BlockSpec((8, TM), lambda i: (0, i)),
        pl.BlockSpec((8, TM), lambda i: (0, i)),
        pl.BlockSpec((8, TM), lambda i: (0, i)),
        pl.BlockSpec((N_EXPERTS, LANES), const),
    ]
    shapes = [
        jax.ShapeDtypeStruct((n_tok, D), F32),
        jax.ShapeDtypeStruct((n_tok, D), F32),
        jax.ShapeDtypeStruct((8, n_tok), jnp.int32),
        jax.ShapeDtypeStruct((8, n_tok), jnp.int32),
        jax.ShapeDtypeStruct((8, n_tok), F32),
        jax.ShapeDtypeStruct((N_EXPERTS, LANES), F32),
    ]
    return specs, shapes


def _moe_front(x1, mod_ref, n2_ref, rw_ref, rb_ref, carry_ref, x1_ref, h2_ref, idx_ref, rank_ref, gate_ref,
               cnt_ref):
    @pl.when(pl.program_id(0) == 0)
    def _():
        carry_ref[...] = jnp.zeros_like(carry_ref)

    x1_ref[...] = x1
    h2 = _prenorm(x1, n2_ref[...], mod_ref[4:5, :], mod_ref[3:4, :])
    h2_ref[...] = h2
    _route(h2, rw_ref, rb_ref, carry_ref, idx_ref, rank_ref, gate_ref, cnt_ref)


def _conv_body(n_ctx_tiles, x_ref, mod_ref, n1_ref, win_ref, cw_ref, wout_ref, n2_ref, rw_ref, rb_ref,
               x1_ref, h2_ref, idx_ref, rank_ref, gate_ref, cnt_ref, carry_ref):
    i = pl.program_id(0)
    row_len = jnp.where(i < n_ctx_tiles, CTX_ROW, GRID_W)
    x = x_ref[...]
    h = _prenorm(x, n1_ref[...], mod_ref[1:2, :], mod_ref[0:1, :])
    proj = _bdot(h, win_ref[...])
    b, cg, u = proj[:, :D], proj[:, D:2 * D], proj[:, 2 * D:]
    mix = _bdot(b * _short_conv(cg * u, cw_ref, row_len), wout_ref[...])
    x1 = x + mod_ref[2:3, :] * mix
    _moe_front(x1, mod_ref, n2_ref, rw_ref, rb_ref, carry_ref, x1_ref, h2_ref, idx_ref, rank_ref, gate_ref,
               cnt_ref)


def _conv_layer(x, mod, n1, win, cw, wout, n2, rw, rb, n_ctx_tiles, tiles_per_seq):
    n_tok = x.shape[0]
    tok, const, modm = _tile_maps(n_ctx_tiles, tiles_per_seq)
    out_specs, out_shapes = _route_outs(n_tok, tok, const)
    return pl.pallas_call(
        functools.partial(_conv_body, n_ctx_tiles),
        grid=(n_tok // TM,),
        in_specs=[
            pl.BlockSpec((TM, D), tok),
            pl.BlockSpec((None, 6, D), modm),
            pl.BlockSpec((1, D), const),
            pl.BlockSpec((D, 3 * D), const),
            pl.BlockSpec((3, D), const),
            pl.BlockSpec((D, D), const),
        ] + _route_specs(const),
        out_specs=out_specs,
        out_shape=out_shapes,
        scratch_shapes=[pltpu.VMEM((N_EXPERTS, LANES), F32)],
        compiler_params=_params(("arbitrary",)),
        name="conv_layer",
    )(x, mod, n1, win, cw, wout, n2, rw, rb)


def _scatter_body(dest_ref, pend_ref, h_ref, xs_ref, zbuf, sem, zsem):
    i = pl.program_id(0)
    n_tok = pl.num_programs(0) * TM

    @pl.when(i == 0)
    def _():
        zbuf[...] = jnp.zeros_like(zbuf)

        def zero_block(j):
            return pltpu.make_async_copy(zbuf, xs_ref.at[pl.ds(pl.multiple_of(j * BLK, BLK), BLK)], zsem)

        def start(j, carry):
            zero_block(j).start()
            return carry

        def wait(j, carry):
            zero_block(j).wait()
            return carry

        n_used = pend_ref[N_EXPERTS - 1] // BLK
        lax.fori_loop(n_used, xs_ref.shape[0] // BLK, start, 0)
        lax.fori_loop(n_used, xs_ref.shape[0] // BLK, wait, 0)
        for phase in range(2):
            for e in range(N_EXPERTS):
                end = pend_ref[e]
                begin = pend_ref[e - 1] if e else 0

                @pl.when(end > begin)
                def _():
                    tail = pl.ds(pl.multiple_of(end - BLK, BLK), BLK)
                    cp = pltpu.make_async_copy(zbuf, xs_ref.at[tail], zsem)
                    if phase == 0:
                        cp.start()
                    else:
                        cp.wait()

    base = i * TM

    def row_copy(t, k):
        return pltpu.make_async_copy(h_ref.at[pl.ds(t, 1)], xs_ref.at[pl.ds(dest_ref[k * n_tok + base + t], 1)],
                                     sem)

    def issue(t, carry):
        for k in range(TOP_K):
            row_copy(t, k).start()
        return carry

    lax.fori_loop(0, TM, issue, 0, unroll=8)
    for _ in range(TOP_K):
        pltpu.make_async_copy(h_ref, xs_ref.at[pl.ds(0, TM)], sem).wait()


def _scatter(dest, pend, h2, n_slots):
    n_tok = h2.shape[0]
    return pl.pallas_call(
        _scatter_body,
        grid_spec=pltpu.PrefetchScalarGridSpec(
            num_scalar_prefetch=2,
            grid=(n_tok // TM,),
            in_specs=[pl.BlockSpec((TM, D), lambda i, *_: (i, 0))],
            out_specs=pl.BlockSpec(memory_space=pl.ANY),
            scratch_shapes=[pltpu.VMEM((BLK, D), F32), pltpu.SemaphoreType.DMA, pltpu.SemaphoreType.DMA],
        ),
        out_shape=jax.ShapeDtypeStruct((n_slots, D), F32),
        compiler_params=_params(("arbitrary",)),
        name="moe_scatter",
    )(dest, pend, h2)


W_ROWS = 128


def _expert_body(be_ref, nu_ref, x_ref, wgu_ref, bgu_ref, wd_ref, bd_ref, y_ref, wgu_bf, wd_bf):
    j = pl.program_id(0)
    n_used = nu_ref[0]
    jj = jnp.minimum(j, n_used - 1)
    e = be_ref[jj]
    e_prev = be_ref[jnp.maximum(jj - 1, 0)]

    @pl.when((j == 0) | (e != e_prev))
    def _():
        def cast(r, carry):
            rows = pl.ds(pl.multiple_of(r * W_ROWS, W_ROWS), W_ROWS)
            wgu_bf[rows, :] = wgu_ref[rows, :].astype(BF16)
            wd_bf[rows, :] = wd_ref[rows, :].astype(BF16)
            return carry

        lax.fori_loop(0, D // W_ROWS, cast, 0)

    @pl.when(j < n_used)
    def _():
        hh = jnp.dot(x_ref[...].astype(BF16), wgu_bf[...], preferred_element_type=F32) + bgu_ref[...]
        hg = jnp.minimum(hh[:, :D_FF], SWIGLU_LIMIT)
        hl = jnp.clip(hh[:, D_FF:], -SWIGLU_LIMIT, SWIGLU_LIMIT)
        act = hg * jax.nn.sigmoid(SWIGLU_ALPHA * hg) * (hl + 1.0)
        y_ref[...] = jnp.dot(act.astype(BF16), wd_bf[...], preferred_element_type=F32) + bd_ref[...]

    @pl.when(j >= n_used)
    def _():
        y_ref[...] = jnp.zeros_like(y_ref)


def _experts(block_e, n_used, xs, layer, wgu, bgu, wd, bd):
    n_slots = xs.shape[0]
    n_layers = wgu.shape[0]

    def blk(j, be, nu):
        return (jnp.minimum(j, nu[0] - 1), 0)

    def wmap(j, be, nu):
        return (layer, be[jnp.minimum(j, nu[0] - 1)], 0, 0)

    return pl.pallas_call(
        _expert_body,
        grid_spec=pltpu.PrefetchScalarGridSpec(
            num_scalar_prefetch=2,
            grid=(n_slots // BLK,),
            in_specs=[
                pl.BlockSpec((BLK, D), blk),
                pl.BlockSpec((None, None, D, 2 * D_FF), wmap),
                pl.BlockSpec((None, None, 1, 2 * D_FF), wmap),
                pl.BlockSpec((None, None, D_FF, D), wmap),
                pl.BlockSpec((None, None, 1, D), wmap),
            ],
            out_specs=pl.BlockSpec((BLK, D), lambda j, be, nu: (j, 0)),
            scratch_shapes=[pltpu.VMEM((D, 2 * D_FF), BF16), pltpu.VMEM((D_FF, D), BF16)],
        ),
        out_shape=jax.ShapeDtypeStruct((n_slots, D), F32),
        compiler_params=_params(("arbitrary",)),
        name="moe_experts",
    )(block_e, n_used, xs, wgu, bgu.reshape(n_layers, N_EXPERTS, 1, 2 * D_FF), wd,
      bd.reshape(n_layers, N_EXPERTS, 1, D))


def _combine_body(final, dest_ref, x_ref, gate_ref, mod_ref, fnw_ref, y_ref, o_ref, buf, sem):
    i = pl.program_id(0)
    n_tiles = pl.num_programs(0)
    n_tok = n_tiles * TM

    def fetch(tile, slot):
        def issue(t, carry):
            for k in range(TOP_K):
                row = dest_ref[k * n_tok + tile * TM + t]
                pltpu.make_async_copy(y_ref.at[pl.ds(row, 1)], buf.at[slot, k, pl.ds(t, 1)], sem.at[slot]).start()
            return carry

        lax.fori_loop(0, TM, issue, 0, unroll=8)

    @pl.when(i == 0)
    def _():
        fetch(0, 0)

    @pl.when(i + 1 < n_tiles)
    def _():
        fetch(i + 1, (i + 1) % 2)

    slot = i % 2
    for k in range(TOP_K):
        pltpu.make_async_copy(y_ref.at[pl.ds(0, TM)], buf.at[slot, k], sem.at[slot]).wait()
    gate = gate_ref[...].T
    acc = gate[:, 0:1] * buf[slot, 0]
    for k in range(1, TOP_K):
        acc = acc + gate[:, k:k + 1] * buf[slot, k]
    x2 = x_ref[...] + mod_ref[5:6, :] * acc
    if final:
        ms = jnp.mean(x2 * x2, axis=-1, keepdims=True)
        x2 = x2 * lax.rsqrt(ms + EPS) * fnw_ref[...]
    o_ref[...] = x2


def _combine(dest, x1, gate, mod, fnw, y_slots, n_ctx_tiles, tiles_per_seq, final):
    n_tok = x1.shape[0]

    def grp(i):
        return jnp.where(i < n_ctx_tiles, 0, 1 + (i - n_ctx_tiles) // tiles_per_seq)

    return pl.pallas_call(
        functools.partial(_combine_body, final),
        grid_spec=pltpu.PrefetchScalarGridSpec(
            num_scalar_prefetch=1,
            grid=(n_tok // TM,),
            in_specs=[
                pl.BlockSpec((TM, D), lambda i, d: (i, 0)),
                pl.BlockSpec((8, TM), lambda i, d: (0, i)),
                pl.BlockSpec((None, 6, D), lambda i, d: (grp(i), 0, 0)),
                pl.BlockSpec((1, D), lambda i, d: (0, 0)),
                pl.BlockSpec(memory_space=pl.ANY),
            ],
            out_specs=pl.BlockSpec((TM, D), lambda i, d: (i, 0)),
            scratch_shapes=[pltpu.VMEM((2, TOP_K, TM, D), F32), pltpu.SemaphoreType.DMA((2,))],
        ),
        out_shape=jax.ShapeDtypeStruct((n_tok, D), F32),
        compiler_params=_params(("arbitrary",)),
        name="moe_combine",
    )(dest, x1, gate, mod, fnw, y_slots)


def _moe(x1, h2, idx, rank, gate, counts, mod, fnw, layer, wgu, bgu, wd, bd, n_ctx_tiles, tiles_per_seq, final):
    n_tok = x1.shape[0]
    n_slots = n_tok * TOP_K + N_EXPERTS * BLK
    cnt = counts[:, 0].astype(jnp.int32)
    padded = (cnt + BLK - 1) // BLK * BLK
    pend = jnp.cumsum(padded)
    pstart = pend - padded
    hit = idx[:TOP_K, None, :] == jnp.arange(N_EXPERTS, dtype=jnp.int32)[None, :, None]
    dest = (jnp.sum(jnp.where(hit, pstart[None, :, None], 0), axis=1) + rank[:TOP_K]).reshape(-1).astype(jnp.int32)
    n_blocks = n_slots // BLK
    block_start = jnp.arange(n_blocks, dtype=jnp.int32)[:, None] * BLK
    block_e = jnp.minimum(jnp.sum(pend[None, :] <= block_start, axis=1), N_EXPERTS - 1).astype(jnp.int32)
    n_used = (pend[-1:] // BLK).astype(jnp.int32)
    xs = _scatter(dest, pend.astype(jnp.int32), h2, n_slots)
    ys = _experts(block_e, n_used, xs, layer, wgu, bgu, wd, bd)
    return _combine(dest, x1, gate, mod, fnw, ys, n_ctx_tiles, tiles_per_seq, final)


def _dn_in_body(n_ctx_tiles, x_ref, mod_ref, n1_ref, w_ref, wba_ref, cw_ref, alog_ref, dtb_ref,
                q_ref, k_ref, v_ref, z_ref, bg_ref):
    i = pl.program_id(0)
    row_len = jnp.where(i < n_ctx_tiles, CTX_ROW, GRID_W)
    h = _prenorm(x_ref[...], n1_ref[...], mod_ref[1:2, :], mod_ref[0:1, :]).astype(BF16)
    proj = jnp.dot(h, w_ref[...], preferred_element_type=F32)
    z_ref[...] = proj[:, CONV_CH:]
    qkv = _silu(_short_conv(proj[:, :CONV_CH], cw_ref, row_len))
    for hd in range(2 * HK):
        s = qkv[:, hd * DK:(hd + 1) * DK]
        n = s * lax.rsqrt(jnp.sum(s * s, axis=-1, keepdims=True) + EPS)
        if hd < HK:
            q_ref[:, hd * DK:(hd + 1) * DK] = n * (DK ** -0.5)
        else:
            k_ref[:, (hd - HK) * DK:(hd - HK + 1) * DK] = n
    v_ref[...] = qkv[:, 2 * QK_DIM:]
    ba = jnp.dot(h, wba_ref[...], preferred_element_type=F32)
    beta = jax.nn.sigmoid(ba)
    a = ba + dtb_ref[...]
    softplus = jnp.maximum(a, 0.0) + jnp.log(1.0 + jnp.exp(-jnp.abs(a)))
    g = -jnp.exp(alog_ref[...]) * softplus
    lane = lax.broadcasted_iota(jnp.int32, (TM, LANES), 1)
    bg_ref[...] = jnp.where(lane < G0, beta, g)


def _dn_in(x, mod, n1, w, wba, cw, alog, dtb, n_ctx_tiles, tiles_per_seq):
    n_tok = x.shape[0]
    tok, const, modm = _tile_maps(n_ctx_tiles, tiles_per_seq)
    return pl.pallas_call(
        functools.partial(_dn_in_body, n_ctx_tiles),
        grid=(n_tok // TM,),
        in_specs=[
            pl.BlockSpec((TM, D), tok),
            pl.BlockSpec((None, 6, D), modm),
            pl.BlockSpec((1, D), const),
            pl.BlockSpec((D, CONV_CH + V_DIM), const),
            pl.BlockSpec((D, LANES), const),
            pl.BlockSpec((3, CONV_CH), const),
            pl.BlockSpec((1, LANES), const),
            pl.BlockSpec((1, LANES), const),
        ],
        out_specs=[
            pl.BlockSpec((TM, QK_DIM), tok),
            pl.BlockSpec((TM, QK_DIM), tok),
            pl.BlockSpec((TM, V_DIM), tok),
            pl.BlockSpec((TM, V_DIM), tok),
            pl.BlockSpec((TM, LANES), tok),
        ],
        out_shape=[
            jax.ShapeDtypeStruct((n_tok, QK_DIM), F32),
            jax.ShapeDtypeStruct((n_tok, QK_DIM), F32),
            jax.ShapeDtypeStruct((n_tok, V_DIM), F32),
            jax.ShapeDtypeStruct((n_tok, V_DIM), F32),
            jax.ShapeDtypeStruct((n_tok, LANES), F32),
        ],
        compiler_params=_params(("arbitrary",)),
        name="dn_in",
    )(x, mod, n1, w, wba, cw, alog, dtb)


def _hdot(a, b):
    return jnp.dot(a, b, preferred_element_type=F32, precision=HIGHEST)


def _nt_dot(a, b):
    return lax.dot_general(a, b, (((1,), (1,)), ((), ())), preferred_element_type=F32)


def _delta_dir(bwd, q_ref, k_ref, v_ref, bg_ref, o_ref, state):
    r2 = lax.broadcasted_iota(jnp.int32, (CHUNK, LANES), 0)
    lane = lax.broadcasted_iota(jnp.int32, (CHUNK, LANES), 1)
    c2 = lane & (CHUNK - 1)
    left = lane < CHUNK
    causal2 = (r2 <= c2) if bwd else (r2 >= c2)
    strict2 = (r2 < c2) if bwd else (r2 > c2)
    eye2 = (r2 == c2).astype(F32)
    lvl = r2 ^ c2
    b0 = HV if bwd else 0
    g0 = 2 * HV + b0
    bg = bg_ref[...]
    gc = _hdot(causal2[:, :CHUNK].astype(F32), bg)
    gct = gc.T
    gtot = jnp.sum(bg, axis=0, keepdims=True)
    egc = jnp.exp(gc)
    eend = jnp.exp(gtot - gc)
    etot = jnp.exp(gtot)

    def pair_cols(a, j):
        return jnp.where(left, a[:, j:j + 1], a[:, j + 1:j + 2])

    def bdiag(x):
        z = jnp.zeros_like(x)
        return jnp.concatenate([jnp.where(left, x, z), jnp.where(left, z, x)], axis=0).astype(BF16)

    eye_k = (lax.broadcasted_iota(jnp.int32, (DK, DK), 0) == lax.broadcasted_iota(jnp.int32, (DK, DK), 1)).astype(BF16)
    ns, attns, kts = [], [], []
    for p in range(HK):
        j = g0 + PAIR * p
        k16 = k_ref[:, p * DK:(p + 1) * DK].astype(BF16)
        q16 = q_ref[:, p * DK:(p + 1) * DK].astype(BF16)
        kq = _nt_dot(jnp.concatenate([k16, q16], axis=0), jnp.concatenate([k16, k16], axis=0))
        grow2 = jnp.concatenate([gct[j:j + 1, :], gct[j + 1:j + 2, :]], axis=1)
        decay2 = jnp.exp(jnp.where(causal2, pair_cols(gc, j) - grow2, NEG))
        ns.append(jnp.where(strict2, kq[:CHUNK] * decay2 * pair_cols(bg, b0 + PAIR * p), 0.0))
        attns.append(kq[CHUNK:] * decay2)
        kts.append(_nt_dot(eye_k, k16).astype(BF16))

    def mdot(a, b):
        return jnp.dot(a.astype(BF16), bdiag(b), preferred_element_type=F32)

    n4 = [jnp.where(lvl < 4, n, 0.0) for n in ns]
    sq = [mdot(a, a) for a in n4]
    ts = [eye2 - a for a in n4]
    ts = [t + mdot(t, s) for t, s in zip(ts, sq)]
    for bit in range(2, 6):
        ys = [mdot(t, jnp.where((lvl >> bit) == 1, n, 0.0)) for t, n in zip(ts, ns)]
        ts = [t - mdot(y, t) for t, y in zip(ts, ys)]

    sols = []
    for p in range(HK):
        kh = k_ref[:, p * DK:(p + 1) * DK]
        rhs = []
        for e in range(PAIR):
            h = PAIR * p + e
            beta = bg[:, b0 + h:b0 + h + 1]
            rhs.append(jnp.concatenate(
                [v_ref[:, h * DV:(h + 1) * DV] * beta, kh * (beta * egc[:, g0 + h:g0 + h + 1])], axis=1))
        sols.append(jnp.dot(bdiag(ts[p]), jnp.concatenate(rhs, axis=0).astype(BF16), preferred_element_type=F32))

    outs = []
    for h in range(HV):
        p, e = divmod(h, PAIR)
        qg = q_ref[:, p * DK:(p + 1) * DK] * egc[:, g0 + h:g0 + h + 1]
        lhs = jnp.concatenate([sols[p][e * CHUNK:(e + 1) * CHUNK, DV:], qg], axis=0)
        outs.append(_bdot(lhs, state[bwd, h]))

    for p in range(HK):
        h0 = PAIR * p
        vns = [sols[p][e * CHUNK:(e + 1) * CHUNK, :DV] - outs[h0 + e][:CHUNK] for e in range(PAIR)]
        intra = jnp.dot(bdiag(attns[p]), jnp.concatenate(vns, axis=0).astype(BF16), preferred_element_type=F32)
        for e in range(PAIR):
            o_ref[:, (h0 + e) * DV:(h0 + e + 1) * DV] = outs[h0 + e][CHUNK:] + intra[e * CHUNK:(e + 1) * CHUNK]
        scaled = jnp.concatenate([vns[e] * eend[:, g0 + h0 + e:g0 + h0 + e + 1] for e in range(PAIR)], axis=1)
        kv = jnp.dot(kts[p], scaled.astype(BF16), preferred_element_type=F32)
        for e in range(PAIR):
            h = h0 + e
            state[bwd, h] = state[bwd, h] * etot[:, g0 + h:g0 + h + 1] + kv[:, e * DV:(e + 1) * DV]


FIRST, LAST, DEC = 1, 2, 4


def _delta_body(fb_ref, bb_ref, si_ref, oi_ref, fl_ref, qf_ref, kf_ref, vf_ref, bgf_ref, qb_ref, kb_ref, vb_ref,
                bgb_ref, s0_ref, of_ref, ob_ref, sout_ref, state):
    flags = fl_ref[pl.program_id(0)]

    @pl.when((flags & FIRST) != 0)
    def _():
        state[...] = jnp.where((flags & DEC) != 0, s0_ref[...], 0.0)

    _delta_dir(0, qf_ref, kf_ref, vf_ref, bgf_ref, of_ref, state)
    _delta_dir(1, qb_ref, kb_ref, vb_ref, bgb_ref, ob_ref, state)

    @pl.when((flags & (LAST | DEC)) == LAST)
    def _():
        sout_ref[...] = state[...]


def _delta(q, k, v, bg, state_delta, n_ctx, ctx_chunks, n_dec, dec_chunks):
    n_tok = q.shape[0]
    fb, bb, si, oi, fl = [], [], [], [], []
    for dec, n_seq, n, base in ((0, n_ctx, ctx_chunks, 0), (1, n_dec, dec_chunks, n_ctx * ctx_chunks)):
        for b in range(n_seq):
            for c in range(n):
                fb.append(base + b * n + c)
                bb.append(base + b * n + n - 1 - c)
                si.append(b if dec else 0)
                oi.append(n_ctx - 1 if dec else b)
                fl.append((FIRST if c == 0 else 0) | (LAST if c == n - 1 else 0) | (DEC if dec else 0))
    tables = [jnp.asarray(t, jnp.int32) for t in (fb, bb, si, oi, fl)]
    fwd = lambda s, fb, bb, si, oi, fl: (fb[s], 0)
    bwd = lambda s, fb, bb, si, oi, fl: (bb[s], 0)
    tok_specs = lambda m: [pl.BlockSpec((CHUNK, QK_DIM), m), pl.BlockSpec((CHUNK, QK_DIM), m),
                           pl.BlockSpec((CHUNK, V_DIM), m), pl.BlockSpec((CHUNK, LANES), m)]
    return pl.pallas_call(
        _delta_body,
        grid_spec=pltpu.PrefetchScalarGridSpec(
            num_scalar_prefetch=5,
            grid=(len(fb),),
            in_specs=tok_specs(fwd) + tok_specs(bwd) + [
                pl.BlockSpec((None, None, 2, HV, DK, DV), lambda s, fb, bb, si, oi, fl: (si[s], 0, 0, 0, 0, 0)),
            ],
            out_specs=[
                pl.BlockSpec((CHUNK, V_DIM), fwd),
                pl.BlockSpec((CHUNK, V_DIM), bwd),
                pl.BlockSpec((None, 2, HV, DK, DV), lambda s, fb, bb, si, oi, fl: (oi[s], 0, 0, 0, 0)),
            ],
            scratch_shapes=[pltpu.VMEM((2, HV, DK, DV), F32)],
        ),
        out_shape=[
            jax.ShapeDtypeStruct((n_tok, V_DIM), F32),
            jax.ShapeDtypeStruct((n_tok, V_DIM), F32),
            jax.ShapeDtypeStruct((n_ctx, 2, HV, DK, DV), F32),
        ],
        compiler_params=_params(("arbitrary",)),
        name="delta_rule",
    )(*tables, q, k, v, bg, q, k, v, bg, state_delta)


def _dn_out_body(of_ref, ob_ref, z_ref, x_ref, mod_ref, nw_ref, wout_ref, n2_ref, rw_ref, rb_ref,
                 x1_ref, h2_ref, idx_ref, rank_ref, gate_ref, cnt_ref, carry_ref):
    o = of_ref[...] + ob_ref[...]
    z = z_ref[...]
    parts = []
    for h in range(HV):
        oh = o[:, h * DV:(h + 1) * DV]
        nh = oh * lax.rsqrt(jnp.mean(oh * oh, axis=-1, keepdims=True) + EPS) * nw_ref[...]
        parts.append((nh * _silu(z[:, h * DV:(h + 1) * DV])).astype(BF16))
    mix = jnp.dot(jnp.concatenate(parts, axis=1), wout_ref[...], preferred_element_type=F32)
    x1 = x_ref[...] + mod_ref[2:3, :] * mix
    _moe_front(x1, mod_ref, n2_ref, rw_ref, rb_ref, carry_ref, x1_ref, h2_ref, idx_ref, rank_ref, gate_ref,
               cnt_ref)


def _dn_out(o_f, o_b, z, x, mod, nw, wout, n2, rw, rb, n_ctx_tiles, tiles_per_seq):
    n_tok = x.shape[0]
    tok, const, modm = _tile_maps(n_ctx_tiles, tiles_per_seq)
    out_specs, out_shapes = _route_outs(n_tok, tok, const)
    return pl.pallas_call(
        _dn_out_body,
        grid=(n_tok // TM,),
        in_specs=[
            pl.BlockSpec((TM, V_DIM), tok),
            pl.BlockSpec((TM, V_DIM), tok),
            pl.BlockSpec((TM, V_DIM), tok),
            pl.BlockSpec((TM, D), tok),
            pl.BlockSpec((None, 6, D), modm),
            pl.BlockSpec((1, DV), const),
            pl.BlockSpec((V_DIM, D), const),
        ] + _route_specs(const),
        out_specs=out_specs,
        out_shape=out_shapes,
        scratch_shapes=[pltpu.VMEM((N_EXPERTS, LANES), F32)],
        compiler_params=_params(("arbitrary",)),
        name="dn_out",
    )(o_f, o_b, z, x, mod, nw, wout, n2, rw, rb)


def kernel(x_prompt, x_sample, state_delta, c, c_ctx, ada_w, ada_b, norm1_w, norm2_w, conv_in_w, conv_w,
           conv_out_w, dn_in_w, dn_conv_w, dn_a_log, dn_dt_bias, dn_norm_w, dn_out_w, router_w, router_b,
           exp_gu_w, exp_gu_b, exp_down_w, exp_down_b, final_norm_w):
    n_ctx, ctx_len, _ = x_prompt.shape
    n_dec, dec_len, _ = x_sample.shape
    assert ctx_len == CTX_ROW and dec_len % TM == 0 and n_dec + 1 <= N_COND
    n_ctx_tok = n_ctx * ctx_len
    n_ctx_tiles = n_ctx_tok // TM
    tiles_per_seq = dec_len // TM

    x = jnp.concatenate([x_prompt.reshape(n_ctx_tok, D), x_sample.reshape(n_dec * dec_len, D)], axis=0)
    cond = jnp.zeros((N_COND, D), F32).at[0].set(c_ctx).at[1:1 + n_dec].set(c)
    mod = _ada(cond, ada_w, ada_b)

    def router(l):
        return norm2_w[l][None, :], router_w[l].T, router_b[l][:, None]

    def experts(l):
        return l, exp_gu_w, exp_gu_b, exp_down_w, exp_down_b

    front = _conv_layer(x, mod[0], norm1_w[0][None, :], conv_in_w[0].astype(BF16), conv_w[0],
                        conv_out_w[0].astype(BF16), *router(0), n_ctx_tiles, tiles_per_seq)
    x = _moe(*front, mod[0], final_norm_w[None, :], *experts(0), n_ctx_tiles, tiles_per_seq, False)

    w_in = dn_in_w[0]
    w_ba = jnp.zeros((D, LANES), F32).at[:, :4 * HV].set(w_in[:, CONV_CH + V_DIM:]).astype(BF16)
    lane_pad = lambda a: jnp.zeros((1, LANES), F32).at[0, 2 * HV:4 * HV].set(a.reshape(-1))
    q, k, v, z, bg = _dn_in(x, mod[1], norm1_w[1][None, :], w_in[:, :CONV_CH + V_DIM].astype(BF16), w_ba,
                            dn_conv_w[0], lane_pad(dn_a_log[0]), lane_pad(dn_dt_bias[0]), n_ctx_tiles,
                            tiles_per_seq)
    o_f, o_b, s_ctx = _delta(q, k, v, bg, state_delta, n_ctx, ctx_len // CHUNK, n_dec, dec_len // CHUNK)
    front = _dn_out(o_f, o_b, z, x, mod[1], dn_norm_w[0][None, :], dn_out_w[0].astype(BF16), *router(1),
                    n_ctx_tiles, tiles_per_seq)
    y = _moe(*front, mod[1], final_norm_w[None, :], *experts(1), n_ctx_tiles, tiles_per_seq, True)

    y_prompt = y[:n_ctx_tok].reshape(n_ctx, ctx_len, D)
    y_sample = y[n_ctx_tok:].reshape(n_dec, dec_len, D)
    return y_prompt, y_sample, s_ctx[:, None]
```

```python
import functools

import jax
import jax.numpy as jnp
from jax import lax
from jax.experimental import pallas as pl
from jax.experimental.pallas import tpu as pltpu

F32 = jnp.float32
BF16 = jnp.bfloat16

D = 1024
TM = 256
CTX_ROW = 256
GRID_W = 64
HK, HV, DK, DV = 8, 16, 128, 128
QK_DIM = HK * DK
V_DIM = HV * DV
CONV_CH = 2 * QK_DIM + V_DIM
CHUNK = 64
N_EXPERTS = 32
TOP_K = 4
D_FF = 1024
SWIGLU_ALPHA = 1.702
SWIGLU_LIMIT = 7.0
EPS = 1e-6
LANES = 128
N_COND = 8
BLK = 512
NEG = -3.0e38
VMEM_LIMIT = 56 * 1024 * 1024
HIGHEST = lax.Precision.HIGHEST
G0 = 2 * HV
PAIR = HV // HK


def _silu(x):
    return x * jax.nn.sigmoid(x)


def _bdot(a, b):
    return jnp.dot(a.astype(BF16), b.astype(BF16), preferred_element_type=F32)


def _params(sem):
    return pltpu.CompilerParams(dimension_semantics=sem, vmem_limit_bytes=VMEM_LIMIT)


def _ada_body(c_ref, w_ref, b_ref, o_ref):
    o_ref[...] = _bdot(_silu(c_ref[...]), w_ref[...]) + b_ref[...]


def _ada(cond, ada_w, ada_b):
    n_layers = ada_w.shape[0]
    tn = 1024
    out = pl.pallas_call(
        _ada_body,
        grid=(n_layers, 6 * D // tn),
        in_specs=[
            pl.BlockSpec((N_COND, D), lambda l, j: (0, 0)),
            pl.BlockSpec((None, D, tn), lambda l, j: (l, 0, j)),
            pl.BlockSpec((None, 1, tn), lambda l, j: (l, 0, j)),
        ],
        out_specs=pl.BlockSpec((None, N_COND, tn), lambda l, j: (l, 0, j)),
        out_shape=jax.ShapeDtypeStruct((n_layers, N_COND, 6 * D), F32),
        compiler_params=_params(("arbitrary", "arbitrary")),
        name="ada",
    )(cond, ada_w, ada_b.reshape(n_layers, 1, 6 * D))
    return out.reshape(n_layers, N_COND, 6, D)


def _prenorm(x, nw, sc, sh):
    ms = jnp.mean(x * x, axis=-1, keepdims=True)
    return (x * lax.rsqrt(ms + EPS) * nw) * (1.0 + sc) + sh


def _short_conv(v, w_ref, row_len):
    n = v.shape[0]
    pos = lax.broadcasted_iota(jnp.int32, (n, 1), 0) & (row_len - 1)
    prev = jnp.where(pos == 0, 0.0, pltpu.roll(v, 1, 0))
    nxt = jnp.where(pos == row_len - 1, 0.0, pltpu.roll(v, n - 1, 0))
    return prev * w_ref[0:1, :] + v * w_ref[1:2, :] + nxt * w_ref[2:3, :]


RT = 1024


def _route_body(h_ref, rwt_ref, rb_ref, idx_ref, rank_ref, gate_ref, cnt_ref, carry_ref):
    @pl.when(pl.program_id(0) == 0)
    def _():
        carry_ref[...] = jnp.zeros_like(carry_ref)

    logits = lax.dot_general(rwt_ref[...], h_ref[...], (((1,), (1,)), ((), ())), preferred_element_type=F32,
                             precision=HIGHEST) + rb_ref[...]
    eid = lax.broadcasted_iota(jnp.int32, (N_EXPERTS, RT), 0)
    work = logits
    sel = jnp.zeros((N_EXPERTS, RT), F32)
    vals, ids, hots = [], [], []
    for _ in range(TOP_K):
        m = jnp.max(work, axis=0, keepdims=True)
        ik = jnp.min(jnp.where(work == m, eid, N_EXPERTS), axis=0, keepdims=True)
        hot = eid == ik
        work = jnp.where(hot, NEG, work)
        sel = sel + hot.astype(F32)
        vals.append(m)
        ids.append(ik)
        hots.append(hot)
    es = [jnp.exp(v - vals[0]) for v in vals]
    den = es[0] + es[1] + es[2] + es[3]
    earlier = (lax.broadcasted_iota(jnp.int32, (TM, TM), 0) < lax.broadcasted_iota(jnp.int32, (TM, TM), 1))
    earlier = earlier.astype(BF16)
    running = carry_ref[:, 0:1]
    ranks = []
    for j in range(RT // TM):
        part = sel[:, j * TM:(j + 1) * TM]
        ranks.append(jnp.dot(part.astype(BF16), earlier, preferred_element_type=F32) + running)
        running = running + jnp.sum(part, axis=1, keepdims=True)
    rank_all = jnp.concatenate(ranks, axis=1)
    slot = lax.broadcasted_iota(jnp.int32, (8, RT), 0)
    idx_o = jnp.zeros((8, RT), jnp.int32)
    rank_o = jnp.zeros((8, RT), jnp.int32)
    gate_o = jnp.zeros((8, RT), F32)
    for k in range(TOP_K):
        rk = jnp.sum(jnp.where(hots[k], rank_all, 0.0), axis=0, keepdims=True)
        idx_o = jnp.where(slot == k, ids[k], idx_o)
        rank_o = jnp.where(slot == k, rk.astype(jnp.int32), rank_o)
        gate_o = jnp.where(slot == k, es[k] / den, gate_o)
    idx_ref[...] = idx_o
    rank_ref[...] = rank_o
    gate_ref[...] = gate_o
    total = jnp.broadcast_to(running, (N_EXPERTS, LANES))
    carry_ref[...] = total
    cnt_ref[...] = total


def _route(h2, rwt, rb):
    n_tok = h2.shape[0]
    const = lambda i: (0, 0)
    cols = lambda i: (0, i)
    return pl.pallas_call(
        _route_body,
        grid=(n_tok // RT,),
        in_specs=[
            pl.BlockSpec((RT, D), lambda i: (i, 0)),
            pl.BlockSpec((N_EXPERTS, D), const),
            pl.BlockSpec((N_EXPERTS, 1), const),
        ],
        out_specs=[
            pl.BlockSpec((8, RT), cols),
            pl.BlockSpec((8, RT), cols),
            pl.BlockSpec((8, RT), cols),
            pl.BlockSpec((N_EXPERTS, LANES), const),
        ],
        out_shape=[
            jax.ShapeDtypeStruct((8, n_tok), jnp.int32),
            jax.ShapeDtypeStruct((8, n_tok), jnp.int32),
            jax.ShapeDtypeStruct((8, n_tok), F32),
            jax.ShapeDtypeStruct((N_EXPERTS, LANES), F32),
        ],
        scratch_shapes=[pltpu.VMEM((N_EXPERTS, LANES), F32)],
        compiler_params=_params(("arbitrary",)),
        name="moe_route",
    )(h2, rwt, rb)


def _tile_maps(n_ctx_tiles, tiles_per_seq):
    def grp(i):
        return jnp.where(i < n_ctx_tiles, 0, 1 + (i - n_ctx_tiles) // tiles_per_seq)

    tok = lambda i: (i, 0)
    const = lambda i: (0, 0)
    mod = lambda i: (grp(i), 0, 0)
    return tok, const, mod


def _mixer_outs(n_tok, tok):
    return ([pl.BlockSpec((TM, D), tok), pl.BlockSpec((TM, D), tok)],
            [jax.ShapeDtypeStruct((n_tok, D), F32), jax.ShapeDtypeStruct((n_tok, D), F32)])


def _moe_input(x1, mod_ref, n2_ref, x1_ref, h2_ref):
    x1_ref[...] = x1
    h2_ref[...] = _prenorm(x1, n2_ref[...], mod_ref[4:5, :], mod_ref[3:4, :])


def _conv_body(n_ctx_tiles, xp_ref, xs_ref, mod_ref, n1_ref, win_ref, cw_ref, wout_ref, n2_ref, x1_ref, h2_ref):
    i = pl.program_id(0)
    row_len = jnp.where(i < n_ctx_tiles, CTX_ROW, GRID_W)
    x = jnp.where(i < n_ctx_tiles, xp_ref[...], xs_ref[...])
    h = _prenorm(x, n1_ref[...], mod_ref[1:2, :], mod_ref[0:1, :])
    proj = _bdot(h, win_ref[...])
    b, cg, u = proj[:, :D], proj[:, D:2 * D], proj[:, 2 * D:]
    mix = _bdot(b * _short_conv(cg * u, cw_ref, row_len), wout_ref[...])
    _moe_input(x + mod_ref[2:3, :] * mix, mod_ref, n2_ref, x1_ref, h2_ref)


def _conv_layer(x_ctx, x_dec, mod, n1, win, cw, wout, n2, n_ctx_tiles, tiles_per_seq):
    n_tok = x_ctx.shape[0] + x_dec.shape[0]
    tok, const, modm = _tile_maps(n_ctx_tiles, tiles_per_seq)
    out_specs, out_shapes = _mixer_outs(n_tok, tok)
    return pl.pallas_call(
        functools.partial(_conv_body, n_ctx_tiles),
        grid=(n_tok // TM,),
        in_specs=[
            pl.BlockSpec((TM, D), lambda i: (jnp.minimum(i, n_ctx_tiles - 1), 0)),
            pl.BlockSpec((TM, D), lambda i: (jnp.maximum(i - n_ctx_tiles, 0), 0)),
            pl.BlockSpec((None, 6, D), modm),
            pl.BlockSpec((1, D), const),
            pl.BlockSpec((D, 3 * D), const),
            pl.BlockSpec((3, D), const),
            pl.BlockSpec((D, D), const),
            pl.BlockSpec((1, D), const),
        ],
        out_specs=out_specs,
        out_shape=out_shapes,
        compiler_params=_params(("arbitrary",)),
        name="conv_layer",
    )(x_ctx, x_dec, mod, n1, win, cw, wout, n2)


def _scatter_body(dest_ref, pend_ref, h_ref, xs_ref, zbuf, sem, zsem):
    i = pl.program_id(0)
    n_tok = pl.num_programs(0) * TM

    @pl.when(i == 0)
    def _():
        zbuf[...] = jnp.zeros_like(zbuf)

        def zero_block(j):
            return pltpu.make_async_copy(zbuf, xs_ref.at[pl.ds(pl.multiple_of(j * BLK, BLK), BLK)], zsem)

        def start(j, carry):
            zero_block(j).start()
            return carry

        def wait(j, carry):
            zero_block(j).wait()
            return carry

        n_used = pend_ref[N_EXPERTS - 1] // BLK
        lax.fori_loop(n_used, xs_ref.shape[0] // BLK, start, 0)
        lax.fori_loop(n_used, xs_ref.shape[0] // BLK, wait, 0)
        for phase in range(2):
            for e in range(N_EXPERTS):
                end = pend_ref[e]
                begin = pend_ref[e - 1] if e else 0

                @pl.when(end > begin)
                def _():
                    tail = pl.ds(pl.multiple_of(end - BLK, BLK), BLK)
                    cp = pltpu.make_async_copy(zbuf, xs_ref.at[tail], zsem)
                    if phase == 0:
                        cp.start()
                    else:
                        cp.wait()

    base = i * TM

    def row_copy(t, k):
        return pltpu.make_async_copy(h_ref.at[pl.ds(t, 1)], xs_ref.at[pl.ds(dest_ref[k * n_tok + base + t], 1)],
                                     sem)

    def issue(t, carry):
        for k in range(TOP_K):
            row_copy(t, k).start()
        return carry

    lax.fori_loop(0, TM, issue, 0, unroll=8)
    for _ in range(TOP_K):
        pltpu.make_async_copy(h_ref, xs_ref.at[pl.ds(0, TM)], sem).wait()


def _scatter(dest, pend, h2, n_slots):
    n_tok = h2.shape[0]
    return pl.pallas_call(
        _scatter_body,
        grid_spec=pltpu.PrefetchScalarGridSpec(
            num_scalar_prefetch=2,
            grid=(n_tok // TM,),
            in_specs=[pl.BlockSpec((TM, D), lambda i, *_: (i, 0))],
            out_specs=pl.BlockSpec(memory_space=pl.ANY),
            scratch_shapes=[pltpu.VMEM((BLK, D), F32), pltpu.SemaphoreType.DMA, pltpu.SemaphoreType.DMA],
        ),
        out_shape=jax.ShapeDtypeStruct((n_slots, D), F32),
        compiler_params=_params(("arbitrary",)),
        name="moe_scatter",
    )(dest, pend, h2)


W_ROWS = 128


def _expert_body(be_ref, nu_ref, x_ref, wgu_ref, bgu_ref, wd_ref, bd_ref, y_ref, wgu_bf, wd_bf):
    j = pl.program_id(0)
    n_used = nu_ref[0]
    jj = jnp.minimum(j, n_used - 1)
    e = be_ref[jj]
    e_prev = be_ref[jnp.maximum(jj - 1, 0)]

    @pl.when((j == 0) | (e != e_prev))
    def _():
        def cast(r, carry):
            rows = pl.ds(pl.multiple_of(r * W_ROWS, W_ROWS), W_ROWS)
            wgu_bf[rows, :] = wgu_ref[rows, :].astype(BF16)
            wd_bf[rows, :] = wd_ref[rows, :].astype(BF16)
            return carry

        lax.fori_loop(0, D // W_ROWS, cast, 0)

    @pl.when(j < n_used)
    def _():
        hh = jnp.dot(x_ref[...].astype(BF16), wgu_bf[...], preferred_element_type=F32) + bgu_ref[...]
        hg = jnp.minimum(hh[:, :D_FF], SWIGLU_LIMIT)
        hl = jnp.clip(hh[:, D_FF:], -SWIGLU_LIMIT, SWIGLU_LIMIT)
        act = hg * jax.nn.sigmoid(SWIGLU_ALPHA * hg) * (hl + 1.0)
        y_ref[...] = jnp.dot(act.astype(BF16), wd_bf[...], preferred_element_type=F32) + bd_ref[...]

    @pl.when(j >= n_used)
    def _():
        y_ref[...] = jnp.zeros_like(y_ref)


def _experts(block_e, n_used, xs, layer, wgu, bgu, wd, bd):
    n_slots = xs.shape[0]
    n_layers = wgu.shape[0]

    def blk(j, be, nu):
        return (jnp.minimum(j, nu[0] - 1), 0)

    def wmap(j, be, nu):
        return (layer, be[jnp.minimum(j, nu[0] - 1)], 0, 0)

    return pl.pallas_call(
        _expert_body,
        grid_spec=pltpu.PrefetchScalarGridSpec(
            num_scalar_prefetch=2,
            grid=(n_slots // BLK,),
            in_specs=[
                pl.BlockSpec((BLK, D), blk),
                pl.BlockSpec((None, None, D, 2 * D_FF), wmap),
                pl.BlockSpec((None, None, 1, 2 * D_FF), wmap),
                pl.BlockSpec((None, None, D_FF, D), wmap),
                pl.BlockSpec((None, None, 1, D), wmap),
            ],
            out_specs=pl.BlockSpec((BLK, D), lambda j, be, nu: (j, 0)),
            scratch_shapes=[pltpu.VMEM((D, 2 * D_FF), BF16), pltpu.VMEM((D_FF, D), BF16)],
        ),
        out_shape=jax.ShapeDtypeStruct((n_slots, D), F32),
        compiler_params=_params(("arbitrary",)),
        name="moe_experts",
    )(block_e, n_used, xs, wgu, bgu.reshape(n_layers, N_EXPERTS, 1, 2 * D_FF), wd,
      bd.reshape(n_layers, N_EXPERTS, 1, D))


def _combine_body(n_ctx_tiles, dest_ref, x_ref, gate_ref, mod_ref, fnw_ref, y_ref, *rest):
    *o_refs, buf, sem = rest
    i = pl.program_id(0)
    n_tiles = pl.num_programs(0)
    n_tok = n_tiles * TM

    def fetch(tile, slot):
        def issue(t, carry):
            for k in range(TOP_K):
                row = dest_ref[k * n_tok + tile * TM + t]
                pltpu.make_async_copy(y_ref.at[pl.ds(row, 1)], buf.at[slot, k, pl.ds(t, 1)], sem.at[slot]).start()
            return carry

        lax.fori_loop(0, TM, issue, 0, unroll=8)

    @pl.when(i == 0)
    def _():
        fetch(0, 0)

    @pl.when(i + 1 < n_tiles)
    def _():
        fetch(i + 1, (i + 1) % 2)

    slot = i % 2
    for k in range(TOP_K):
        pltpu.make_async_copy(y_ref.at[pl.ds(0, TM)], buf.at[slot, k], sem.at[slot]).wait()
    gate = gate_ref[...].T
    acc = gate[:, 0:1] * buf[slot, 0]
    for k in range(1, TOP_K):
        acc = acc + gate[:, k:k + 1] * buf[slot, k]
    x2 = x_ref[...] + mod_ref[5:6, :] * acc
    if n_ctx_tiles is None:
        o_refs[0][...] = x2
    else:
        ms = jnp.mean(x2 * x2, axis=-1, keepdims=True)
        y = x2 * lax.rsqrt(ms + EPS) * fnw_ref[...]

        @pl.when(i < n_ctx_tiles)
        def _():
            o_refs[0][...] = y

        @pl.when(i >= n_ctx_tiles)
        def _():
            o_refs[1][...] = y


def _combine(dest, x1, gate, mod, fnw, y_slots, n_ctx_tiles, tiles_per_seq, final):
    n_tok = x1.shape[0]

    def grp(i):
        return jnp.where(i < n_ctx_tiles, 0, 1 + (i - n_ctx_tiles) // tiles_per_seq)

    if final:
        out_specs = [pl.BlockSpec((TM, D), lambda i, d: (jnp.minimum(i, n_ctx_tiles - 1), 0)),
                     pl.BlockSpec((TM, D), lambda i, d: (jnp.maximum(i - n_ctx_tiles, 0), 0))]
        out_shape = [jax.ShapeDtypeStruct((n_ctx_tiles * TM, D), F32),
                     jax.ShapeDtypeStruct((n_tok - n_ctx_tiles * TM, D), F32)]
    else:
        out_specs = [pl.BlockSpec((TM, D), lambda i, d: (i, 0))]
        out_shape = [jax.ShapeDtypeStruct((n_tok, D), F32)]
    return pl.pallas_call(
        functools.partial(_combine_body, n_ctx_tiles if final else None),
        grid_spec=pltpu.PrefetchScalarGridSpec(
            num_scalar_prefetch=1,
            grid=(n_tok // TM,),
            in_specs=[
                pl.BlockSpec((TM, D), lambda i, d: (i, 0)),
                pl.BlockSpec((8, TM), lambda i, d: (0, i)),
                pl.BlockSpec((None, 6, D), lambda i, d: (grp(i), 0, 0)),
                pl.BlockSpec((1, D), lambda i, d: (0, 0)),
                pl.BlockSpec(memory_space=pl.ANY),
            ],
            out_specs=out_specs,
            scratch_shapes=[pltpu.VMEM((2, TOP_K, TM, D), F32), pltpu.SemaphoreType.DMA((2,))],
        ),
        out_shape=out_shape,
        compiler_params=_params(("arbitrary",)),
        name="moe_combine",
    )(dest, x1, gate, mod, fnw, y_slots)


def _moe(x1, h2, rwt, rb, mod, fnw, layer, wgu, bgu, wd, bd, n_ctx_tiles, tiles_per_seq, final):
    n_tok = x1.shape[0]
    idx, rank, gate, counts = _route(h2, rwt, rb)
    n_slots = n_tok * TOP_K + N_EXPERTS * BLK
    cnt = counts[:, 0].astype(jnp.int32)
    padded = (cnt + BLK - 1) // BLK * BLK
    pend = jnp.cumsum(padded)
    pstart = pend - padded
    hit = idx[:TOP_K, None, :] == jnp.arange(N_EXPERTS, dtype=jnp.int32)[None, :, None]
    dest = (jnp.sum(jnp.where(hit, pstart[None, :, None], 0), axis=1) + rank[:TOP_K]).reshape(-1).astype(jnp.int32)
    n_blocks = n_slots // BLK
    block_start = jnp.arange(n_blocks, dtype=jnp.int32)[:, None] * BLK
    block_e = jnp.minimum(jnp.sum(pend[None, :] <= block_start, axis=1), N_EXPERTS - 1).astype(jnp.int32)
    n_used = (pend[-1:] // BLK).astype(jnp.int32)
    xs = _scatter(dest, pend.astype(jnp.int32), h2, n_slots)
    ys = _experts(block_e, n_used, xs, layer, wgu, bgu, wd, bd)
    return _combine(dest, x1, gate, mod, fnw, ys, n_ctx_tiles, tiles_per_seq, final)


def _dn_in_body(n_ctx_tiles, x_ref, mod_ref, n1_ref, w_ref, wba_ref, cw_ref, alog_ref, dtb_ref,
                q_ref, k_ref, v_ref, z_ref, bg_ref):
    i = pl.program_id(0)
    row_len = jnp.where(i < n_ctx_tiles, CTX_ROW, GRID_W)
    h = _prenorm(x_ref[...], n1_ref[...], mod_ref[1:2, :], mod_ref[0:1, :]).astype(BF16)
    proj = jnp.dot(h, w_ref[...], preferred_element_type=F32)
    z_ref[...] = proj[:, CONV_CH:]
    qkv = _silu(_short_conv(proj[:, :CONV_CH], cw_ref, row_len))
    for hd in range(2 * HK):
        s = qkv[:, hd * DK:(hd + 1) * DK]
        n = s * lax.rsqrt(jnp.sum(s * s, axis=-1, keepdims=True) + EPS)
        if hd < HK:
            q_ref[:, hd * DK:(hd + 1) * DK] = n * (DK ** -0.5)
        else:
            k_ref[:, (hd - HK) * DK:(hd - HK + 1) * DK] = n
    v_ref[...] = qkv[:, 2 * QK_DIM:]
    ba = jnp.dot(h, wba_ref[...], preferred_element_type=F32)
    beta = jax.nn.sigmoid(ba)
    a = ba + dtb_ref[...]
    softplus = jnp.maximum(a, 0.0) + jnp.log(1.0 + jnp.exp(-jnp.abs(a)))
    g = -jnp.exp(alog_ref[...]) * softplus
    lane = lax.broadcasted_iota(jnp.int32, (TM, LANES), 1)
    bg_ref[...] = jnp.where(lane < G0, beta, g)


def _dn_in(x, mod, n1, w, wba, cw, alog, dtb, n_ctx_tiles, tiles_per_seq):
    n_tok = x.shape[0]
    tok, const, modm = _tile_maps(n_ctx_tiles, tiles_per_seq)
    return pl.pallas_call(
        functools.partial(_dn_in_body, n_ctx_tiles),
        grid=(n_tok // TM,),
        in_specs=[
            pl.BlockSpec((TM, D), tok),
            pl.BlockSpec((None, 6, D), modm),
            pl.BlockSpec((1, D), const),
            pl.BlockSpec((D, CONV_CH + V_DIM), const),
            pl.BlockSpec((D, LANES), const),
            pl.BlockSpec((3, CONV_CH), const),
            pl.BlockSpec((1, LANES), const),
            pl.BlockSpec((1, LANES), const),
        ],
        out_specs=[
            pl.BlockSpec((TM, QK_DIM), tok),
            pl.BlockSpec((TM, QK_DIM), tok),
            pl.BlockSpec((TM, V_DIM), tok),
            pl.BlockSpec((TM, V_DIM), tok),
            pl.BlockSpec((TM, LANES), tok),
        ],
        out_shape=[
            jax.ShapeDtypeStruct((n_tok, QK_DIM), F32),
            jax.ShapeDtypeStruct((n_tok, QK_DIM), F32),
            jax.ShapeDtypeStruct((n_tok, V_DIM), F32),
            jax.ShapeDtypeStruct((n_tok, V_DIM), F32),
            jax.ShapeDtypeStruct((n_tok, LANES), F32),
        ],
        compiler_params=_params(("arbitrary",)),
        name="dn_in",
    )(x, mod, n1, w, wba, cw, alog, dtb)


def _hdot(a, b):
    return jnp.dot(a, b, preferred_element_type=F32, precision=HIGHEST)


def _nt_dot(a, b):
    return lax.dot_general(a, b, (((1,), (1,)), ((), ())), preferred_element_type=F32)


def _delta_dir(bwd, q_ref, k_ref, v_ref, bg_ref, o_ref, state):
    r2 = lax.broadcasted_iota(jnp.int32, (CHUNK, LANES), 0)
    lane = lax.broadcasted_iota(jnp.int32, (CHUNK, LANES), 1)
    c2 = lane & (CHUNK - 1)
    left = lane < CHUNK
    causal2 = (r2 <= c2) if bwd else (r2 >= c2)
    strict2 = (r2 < c2) if bwd else (r2 > c2)
    eye2 = (r2 == c2).astype(F32)
    lvl = r2 ^ c2
    b0 = HV if bwd else 0
    g0 = 2 * HV + b0
    bg = bg_ref[...]
    gc = _hdot(causal2[:, :CHUNK].astype(F32), bg)
    gct = gc.T
    gtot = jnp.sum(bg, axis=0, keepdims=True)
    egc = jnp.exp(gc)
    eend = jnp.exp(gtot - gc)
    etot = jnp.exp(gtot)

    def pair_cols(a, j):
        return jnp.where(left, a[:, j:j + 1], a[:, j + 1:j + 2])

    def bdiag(x):
        z = jnp.zeros_like(x)
        return jnp.concatenate([jnp.where(left, x, z), jnp.where(left, z, x)], axis=0).astype(BF16)

    eye_k = (lax.broadcasted_iota(jnp.int32, (DK, DK), 0) == lax.broadcasted_iota(jnp.int32, (DK, DK), 1)).astype(BF16)
    ns, attns, kts = [], [], []
    for p in range(HK):
        j = g0 + PAIR * p
        k16 = k_ref[:, p * DK:(p + 1) * DK].astype(BF16)
        q16 = q_ref[:, p * DK:(p + 1) * DK].astype(BF16)
        kq = _nt_dot(jnp.concatenate([k16, q16], axis=0), jnp.concatenate([k16, k16], axis=0))
        grow2 = jnp.concatenate([gct[j:j + 1, :], gct[j + 1:j + 2, :]], axis=1)
        decay2 = jnp.exp(jnp.where(causal2, pair_cols(gc, j) - grow2, NEG))
        ns.append(jnp.where(strict2, kq[:CHUNK] * decay2 * pair_cols(bg, b0 + PAIR * p), 0.0))
        attns.append(kq[CHUNK:] * decay2)
        kts.append(_nt_dot(eye_k, k16).astype(BF16))

    def mdot(a, b):
        return jnp.dot(a.astype(BF16), bdiag(b), preferred_element_type=F32)

    n4 = [jnp.where(lvl < 4, n, 0.0) for n in ns]
    sq = [mdot(a, a) for a in n4]
    ts = [eye2 - a for a in n4]
    ts = [t + mdot(t, s) for t, s in zip(ts, sq)]
    for bit in range(2, 6):
        ys = [mdot(t, jnp.where((lvl >> bit) == 1, n, 0.0)) for t, n in zip(ts, ns)]
        ts = [t - mdot(y, t) for t, y in zip(ts, ys)]

    sols = []
    for p in range(HK):
        kh = k_ref[:, p * DK:(p + 1) * DK]
        rhs = []
        for e in range(PAIR):
            h = PAIR * p + e
            beta = bg[:, b0 + h:b0 + h + 1]
            rhs.append(jnp.concatenate(
                [v_ref[:, h * DV:(h + 1) * DV] * beta, kh * (beta * egc[:, g0 + h:g0 + h + 1])], axis=1))
        sols.append(jnp.dot(bdiag(ts[p]), jnp.concatenate(rhs, axis=0).astype(BF16), preferred_element_type=F32))

    outs = []
    for h in range(HV):
        p, e = divmod(h, PAIR)
        qg = q_ref[:, p * DK:(p + 1) * DK] * egc[:, g0 + h:g0 + h + 1]
        lhs = jnp.concatenate([sols[p][e * CHUNK:(e + 1) * CHUNK, DV:], qg], axis=0)
        outs.append(_bdot(lhs, state[bwd, h]))

    for p in range(HK):
        h0 = PAIR * p
        vns = [sols[p][e * CHUNK:(e + 1) * CHUNK, :DV] - outs[h0 + e][:CHUNK] for e in range(PAIR)]
        intra = jnp.dot(bdiag(attns[p]), jnp.concatenate(vns, axis=0).astype(BF16), preferred_element_type=F32)
        for e in range(PAIR):
            o_ref[:, (h0 + e) * DV:(h0 + e + 1) * DV] = outs[h0 + e][CHUNK:] + intra[e * CHUNK:(e + 1) * CHUNK]
        scaled = jnp.concatenate([vns[e] * eend[:, g0 + h0 + e:g0 + h0 + e + 1] for e in range(PAIR)], axis=1)
        kv = jnp.dot(kts[p], scaled.astype(BF16), preferred_element_type=F32)
        for e in range(PAIR):
            h = h0 + e
            state[bwd, h] = state[bwd, h] * etot[:, g0 + h:g0 + h + 1] + kv[:, e * DV:(e + 1) * DV]


FIRST, LAST, DEC = 1, 2, 4


def _delta_body(fb_ref, bb_ref, si_ref, oi_ref, fl_ref, qf_ref, kf_ref, vf_ref, bgf_ref, qb_ref, kb_ref, vb_ref,
                bgb_ref, s0_ref, of_ref, ob_ref, sout_ref, state):
    flags = fl_ref[pl.program_id(0)]

    @pl.when((flags & FIRST) != 0)
    def _():
        state[...] = jnp.where((flags & DEC) != 0, s0_ref[...], 0.0)

    _delta_dir(0, qf_ref, kf_ref, vf_ref, bgf_ref, of_ref, state)
    _delta_dir(1, qb_ref, kb_ref, vb_ref, bgb_ref, ob_ref, state)

    @pl.when((flags & (LAST | DEC)) == LAST)
    def _():
        sout_ref[...] = state[...]


def _delta(q, k, v, bg, state_delta, n_ctx, ctx_chunks, n_dec, dec_chunks):
    n_tok = q.shape[0]
    fb, bb, si, oi, fl = [], [], [], [], []
    for dec, n_seq, n, base in ((0, n_ctx, ctx_chunks, 0), (1, n_dec, dec_chunks, n_ctx * ctx_chunks)):
        for b in range(n_seq):
            for c in range(n):
                fb.append(base + b * n + c)
                bb.append(base + b * n + n - 1 - c)
                si.append(b if dec else 0)
                oi.append(n_ctx - 1 if dec else b)
                fl.append((FIRST if c == 0 else 0) | (LAST if c == n - 1 else 0) | (DEC if dec else 0))
    tables = [jnp.asarray(t, jnp.int32) for t in (fb, bb, si, oi, fl)]
    fwd = lambda s, fb, bb, si, oi, fl: (fb[s], 0)
    bwd = lambda s, fb, bb, si, oi, fl: (bb[s], 0)
    tok_specs = lambda m: [pl.BlockSpec((CHUNK, QK_DIM), m), pl.BlockSpec((CHUNK, QK_DIM), m),
                           pl.BlockSpec((CHUNK, V_DIM), m), pl.BlockSpec((CHUNK, LANES), m)]
    return pl.pallas_call(
        _delta_body,
        grid_spec=pltpu.PrefetchScalarGridSpec(
            num_scalar_prefetch=5,
            grid=(len(fb),),
            in_specs=tok_specs(fwd) + tok_specs(bwd) + [
                pl.BlockSpec((None, None, 2, HV, DK, DV), lambda s, fb, bb, si, oi, fl: (si[s], 0, 0, 0, 0, 0)),
            ],
            out_specs=[
                pl.BlockSpec((CHUNK, V_DIM), fwd),
                pl.BlockSpec((CHUNK, V_DIM), bwd),
                pl.BlockSpec((None, 2, HV, DK, DV), lambda s, fb, bb, si, oi, fl: (oi[s], 0, 0, 0, 0)),
            ],
            scratch_shapes=[pltpu.VMEM((2, HV, DK, DV), F32)],
        ),
        out_shape=[
            jax.ShapeDtypeStruct((n_tok, V_DIM), F32),
            jax.ShapeDtypeStruct((n_tok, V_DIM), F32),
            jax.ShapeDtypeStruct((n_ctx, 2, HV, DK, DV), F32),
        ],
        compiler_params=_params(("arbitrary",)),
        name="delta_rule",
    )(*tables, q, k, v, bg, q, k, v, bg, state_delta)


def _dn_out_body(of_ref, ob_ref, z_ref, x_ref, mod_ref, nw_ref, wout_ref, n2_ref, x1_ref, h2_ref):
    o = of_ref[...] + ob_ref[...]
    z = z_ref[...]
    parts = []
    for h in range(HV):
        oh = o[:, h * DV:(h + 1) * DV]
        nh = oh * lax.rsqrt(jnp.mean(oh * oh, axis=-1, keepdims=True) + EPS) * nw_ref[...]
        parts.append((nh * _silu(z[:, h * DV:(h + 1) * DV])).astype(BF16))
    mix = jnp.dot(jnp.concatenate(parts, axis=1), wout_ref[...], preferred_element_type=F32)
    _moe_input(x_ref[...] + mod_ref[2:3, :] * mix, mod_ref, n2_ref, x1_ref, h2_ref)


def _dn_out(o_f, o_b, z, x, mod, nw, wout, n2, n_ctx_tiles, tiles_per_seq):
    n_tok = x.shape[0]
    tok, const, modm = _tile_maps(n_ctx_tiles, tiles_per_seq)
    out_specs, out_shapes = _mixer_outs(n_tok, tok)
    return pl.pallas_call(
        _dn_out_body,
        grid=(n_tok // TM,),
        in_specs=[
            pl.BlockSpec((TM, V_DIM), tok),
            pl.BlockSpec((TM, V_DIM), tok),
            pl.BlockSpec((TM, V_DIM), tok),
            pl.BlockSpec((TM, D), tok),
            pl.BlockSpec((None, 6, D), modm),
            pl.BlockSpec((1, DV), const),
            pl.BlockSpec((V_DIM, D), const),
            pl.BlockSpec((1, D), const),
        ],
        out_specs=out_specs,
        out_shape=out_shapes,
        compiler_params=_params(("arbitrary",)),
        name="dn_out",
    )(o_f, o_b, z, x, mod, nw, wout, n2)


def kernel(x_prompt, x_sample, state_delta, c, c_ctx, ada_w, ada_b, norm1_w, norm2_w, conv_in_w, conv_w,
           conv_out_w, dn_in_w, dn_conv_w, dn_a_log, dn_dt_bias, dn_norm_w, dn_out_w, router_w, router_b,
           exp_gu_w, exp_gu_b, exp_down_w, exp_down_b, final_norm_w):
    n_ctx, ctx_len, _ = x_prompt.shape
    n_dec, dec_len, _ = x_sample.shape
    assert ctx_len == CTX_ROW and dec_len % TM == 0 and n_dec + 1 <= N_COND
    n_ctx_tok = n_ctx * ctx_len
    n_ctx_tiles = n_ctx_tok // TM
    tiles_per_seq = dec_len // TM

    cond = jnp.zeros((N_COND, D), F32).at[0].set(c_ctx).at[1:1 + n_dec].set(c)
    mod = _ada(cond, ada_w, ada_b)

    def moe(l, x1, h2, final):
        return _moe(x1, h2, router_w[l].T, router_b[l][:, None], mod[l], final_norm_w[None, :], l, exp_gu_w,
                    exp_gu_b, exp_down_w, exp_down_b, n_ctx_tiles, tiles_per_seq, final)

    x1, h2 = _conv_layer(x_prompt.reshape(n_ctx_tok, D), x_sample.reshape(n_dec * dec_len, D), mod[0],
                         norm1_w[0][None, :], conv_in_w[0].astype(BF16), conv_w[0], conv_out_w[0].astype(BF16),
                         norm2_w[0][None, :], n_ctx_tiles, tiles_per_seq)
    x, = moe(0, x1, h2, False)

    w_in = dn_in_w[0]
    w_ba = jnp.zeros((D, LANES), F32).at[:, :4 * HV].set(w_in[:, CONV_CH + V_DIM:]).astype(BF16)
    lane_pad = lambda a: jnp.zeros((1, LANES), F32).at[0, 2 * HV:4 * HV].set(a.reshape(-1))
    q, k, v, z, bg = _dn_in(x, mod[1], norm1_w[1][None, :], w_in[:, :CONV_CH + V_DIM].astype(BF16), w_ba,
                            dn_conv_w[0], lane_pad(dn_a_log[0]), lane_pad(dn_dt_bias[0]), n_ctx_tiles,
                            tiles_per_seq)
    o_f, o_b, s_ctx = _delta(q, k, v, bg, state_delta, n_ctx, ctx_len // CHUNK, n_dec, dec_len // CHUNK)
    x1, h2 = _dn_out(o_f, o_b, z, x, mod[1], dn_norm_w[0][None, :], dn_out_w[0].astype(BF16),
                     norm2_w[1][None, :], n_ctx_tiles, tiles_per_seq)
    y_ctx, y_dec = moe(1, x1, h2, True)
    return y_ctx.reshape(n_ctx, ctx_len, D), y_dec.reshape(n_dec, dec_len, D), s_ctx[:, None]
```

```python
import functools

import jax
import jax.numpy as jnp
from jax import lax
from jax.experimental import pallas as pl
from jax.experimental.pallas import tpu as pltpu

F32 = jnp.float32
BF16 = jnp.bfloat16

D = 1024
TM = 256
CTX_ROW = 256
GRID_W = 64
HK, HV, DK, DV = 8, 16, 128, 128
QK_DIM = HK * DK
V_DIM = HV * DV
CONV_CH = 2 * QK_DIM + V_DIM
CHUNK = 64
N_EXPERTS = 32
TOP_K = 4
D_FF = 1024
SWIGLU_ALPHA = 1.702
SWIGLU_LIMIT = 7.0
EPS = 1e-6
LANES = 128
N_COND = 8
BLK = 512
NEG = -3.0e38
VMEM_LIMIT = 56 * 1024 * 1024
HIGHEST = lax.Precision.HIGHEST
G0 = 2 * HV
PAIR = HV // HK


def _silu(x):
    return x * jax.nn.sigmoid(x)


def _bdot(a, b):
    return jnp.dot(a.astype(BF16), b.astype(BF16), preferred_element_type=F32)


def _params(sem):
    return pltpu.CompilerParams(dimension_semantics=sem, vmem_limit_bytes=VMEM_LIMIT)


def _ada_body(c_ref, w_ref, b_ref, o_ref):
    o_ref[...] = _bdot(_silu(c_ref[...]), w_ref[...]) + b_ref[...]


def _ada(cond, ada_w, ada_b):
    n_layers = ada_w.shape[0]
    tn = 1024
    out = pl.pallas_call(
        _ada_body,
        grid=(n_layers, 6 * D // tn),
        in_specs=[
            pl.BlockSpec((N_COND, D), lambda l, j: (0, 0)),
            pl.BlockSpec((None, D, tn), lambda l, j: (l, 0, j)),
            pl.BlockSpec((None, 1, tn), lambda l, j: (l, 0, j)),
        ],
        out_specs=pl.BlockSpec((None, N_COND, tn), lambda l, j: (l, 0, j)),
        out_shape=jax.ShapeDtypeStruct((n_layers, N_COND, 6 * D), F32),
        compiler_params=_params(("arbitrary", "arbitrary")),
        name="ada",
    )(cond, ada_w, ada_b.reshape(n_layers, 1, 6 * D))
    return out.reshape(n_layers, N_COND, 6, D)


def _prenorm(x, nw, sc, sh):
    ms = jnp.mean(x * x, axis=-1, keepdims=True)
    return (x * lax.rsqrt(ms + EPS) * nw) * (1.0 + sc) + sh


def _short_conv(v, w_ref, row_len):
    n = v.shape[0]
    pos = lax.broadcasted_iota(jnp.int32, (n, 1), 0) & (row_len - 1)
    prev = jnp.where(pos == 0, 0.0, pltpu.roll(v, 1, 0))
    nxt = jnp.where(pos == row_len - 1, 0.0, pltpu.roll(v, n - 1, 0))
    return prev * w_ref[0:1, :] + v * w_ref[1:2, :] + nxt * w_ref[2:3, :]


RT = 1024


def _route_body(h_ref, rwt_ref, rb_ref, idx_ref, rank_ref, gate_ref, cnt_ref, carry_ref):
    @pl.when(pl.program_id(0) == 0)
    def _():
        carry_ref[...] = jnp.zeros_like(carry_ref)

    logits = lax.dot_general(rwt_ref[...], h_ref[...], (((1,), (1,)), ((), ())), preferred_element_type=F32,
                             precision=HIGHEST) + rb_ref[...]
    eid = lax.broadcasted_iota(jnp.int32, (N_EXPERTS, RT), 0)
    work = logits
    sel = jnp.zeros((N_EXPERTS, RT), F32)
    vals, ids, hots = [], [], []
    for _ in range(TOP_K):
        m = jnp.max(work, axis=0, keepdims=True)
        ik = jnp.min(jnp.where(work == m, eid, N_EXPERTS), axis=0, keepdims=True)
        hot = eid == ik
        work = jnp.where(hot, NEG, work)
        sel = sel + hot.astype(F32)
        vals.append(m)
        ids.append(ik)
        hots.append(hot)
    es = [jnp.exp(v - vals[0]) for v in vals]
    den = es[0] + es[1] + es[2] + es[3]
    earlier = (lax.broadcasted_iota(jnp.int32, (TM, TM), 0) < lax.broadcasted_iota(jnp.int32, (TM, TM), 1))
    earlier = earlier.astype(BF16)
    running = carry_ref[:, 0:1]
    ranks = []
    for j in range(RT // TM):
        part = sel[:, j * TM:(j + 1) * TM]
        ranks.append(jnp.dot(part.astype(BF16), earlier, preferred_element_type=F32) + running)
        running = running + jnp.sum(part, axis=1, keepdims=True)
    rank_all = jnp.concatenate(ranks, axis=1)
    slot = lax.broadcasted_iota(jnp.int32, (8, RT), 0)
    idx_o = jnp.zeros((8, RT), jnp.int32)
    rank_o = jnp.zeros((8, RT), jnp.int32)
    gate_o = jnp.zeros((8, RT), F32)
    for k in range(TOP_K):
        rk = jnp.sum(jnp.where(hots[k], rank_all, 0.0), axis=0, keepdims=True)
        idx_o = jnp.where(slot == k, ids[k], idx_o)
        rank_o = jnp.where(slot == k, rk.astype(jnp.int32), rank_o)
        gate_o = jnp.where(slot == k, es[k] / den, gate_o)
    idx_ref[...] = idx_o
    rank_ref[...] = rank_o
    gate_ref[...] = gate_o
    total = jnp.broadcast_to(running, (N_EXPERTS, LANES))
    carry_ref[...] = total
    cnt_ref[...] = total


def _route(h2, rwt, rb):
    n_tok = h2.shape[0]
    const = lambda i: (0, 0)
    cols = lambda i: (0, i)
    return pl.pallas_call(
        _route_body,
        grid=(n_tok // RT,),
        in_specs=[
            pl.BlockSpec((RT, D), lambda i: (i, 0)),
            pl.BlockSpec((N_EXPERTS, D), const),
            pl.BlockSpec((N_EXPERTS, 1), const),
        ],
        out_specs=[
            pl.BlockSpec((8, RT), cols),
            pl.BlockSpec((8, RT), cols),
            pl.BlockSpec((8, RT), cols),
            pl.BlockSpec((N_EXPERTS, LANES), const),
        ],
        out_shape=[
            jax.ShapeDtypeStruct((8, n_tok), jnp.int32),
            jax.ShapeDtypeStruct((8, n_tok), jnp.int32),
            jax.ShapeDtypeStruct((8, n_tok), F32),
            jax.ShapeDtypeStruct((N_EXPERTS, LANES), F32),
        ],
        scratch_shapes=[pltpu.VMEM((N_EXPERTS, LANES), F32)],
        compiler_params=_params(("arbitrary",)),
        name="moe_route",
    )(h2, rwt, rb)


def _tile_maps(n_ctx_tiles, tiles_per_seq):
    def grp(i):
        return jnp.where(i < n_ctx_tiles, 0, 1 + (i - n_ctx_tiles) // tiles_per_seq)

    tok = lambda i: (i, 0)
    const = lambda i: (0, 0)
    mod = lambda i: (grp(i), 0, 0)
    return tok, const, mod


def _mixer_outs(n_tok, tok):
    return ([pl.BlockSpec((TM, D), tok), pl.BlockSpec((TM, D), tok)],
            [jax.ShapeDtypeStruct((n_tok, D), F32), jax.ShapeDtypeStruct((n_tok, D), F32)])


def _moe_input(x1, mod_ref, n2_ref, x1_ref, h2_ref):
    x1_ref[...] = x1
    h2_ref[...] = _prenorm(x1, n2_ref[...], mod_ref[4:5, :], mod_ref[3:4, :])


def _conv_body(n_ctx_tiles, xp_ref, xs_ref, mod_ref, n1_ref, win_ref, cw_ref, wout_ref, n2_ref, x1_ref, h2_ref):
    i = pl.program_id(0)
    row_len = jnp.where(i < n_ctx_tiles, CTX_ROW, GRID_W)
    x = jnp.where(i < n_ctx_tiles, xp_ref[...], xs_ref[...])
    h = _prenorm(x, n1_ref[...], mod_ref[1:2, :], mod_ref[0:1, :])
    proj = _bdot(h, win_ref[...])
    b, cg, u = proj[:, :D], proj[:, D:2 * D], proj[:, 2 * D:]
    mix = _bdot(b * _short_conv(cg * u, cw_ref, row_len), wout_ref[...])
    _moe_input(x + mod_ref[2:3, :] * mix, mod_ref, n2_ref, x1_ref, h2_ref)


def _conv_layer(x_ctx, x_dec, mod, n1, win, cw, wout, n2, n_ctx_tiles, tiles_per_seq):
    n_tok = x_ctx.shape[0] + x_dec.shape[0]
    tok, const, modm = _tile_maps(n_ctx_tiles, tiles_per_seq)
    out_specs, out_shapes = _mixer_outs(n_tok, tok)
    return pl.pallas_call(
        functools.partial(_conv_body, n_ctx_tiles),
        grid=(n_tok // TM,),
        in_specs=[
            pl.BlockSpec((TM, D), lambda i: (jnp.minimum(i, n_ctx_tiles - 1), 0)),
            pl.BlockSpec((TM, D), lambda i: (jnp.maximum(i - n_ctx_tiles, 0), 0)),
            pl.BlockSpec((None, 6, D), modm),
            pl.BlockSpec((1, D), const),
            pl.BlockSpec((D, 3 * D), const),
            pl.BlockSpec((3, D), const),
            pl.BlockSpec((D, D), const),
            pl.BlockSpec((1, D), const),
        ],
        out_specs=out_specs,
        out_shape=out_shapes,
        compiler_params=_params(("arbitrary",)),
        name="conv_layer",
    )(x_ctx, x_dec, mod, n1, win, cw, wout, n2)


def _scatter_body(dest_ref, pend_ref, h_ref, xs_ref, zbuf, sem, zsem):
    i = pl.program_id(0)
    n_tok = pl.num_programs(0) * TM

    @pl.when(i == 0)
    def _():
        zbuf[...] = jnp.zeros_like(zbuf)

        def zero_block(j):
            return pltpu.make_async_copy(zbuf, xs_ref.at[pl.ds(pl.multiple_of(j * BLK, BLK), BLK)], zsem)

        def start(j, carry):
            zero_block(j).start()
            return carry

        def wait(j, carry):
            zero_block(j).wait()
            return carry

        n_used = pend_ref[N_EXPERTS - 1] // BLK
        lax.fori_loop(n_used, xs_ref.shape[0] // BLK, start, 0)
        lax.fori_loop(n_used, xs_ref.shape[0] // BLK, wait, 0)
        for phase in range(2):
            for e in range(N_EXPERTS):
                end = pend_ref[e]
                begin = pend_ref[e - 1] if e else 0

                @pl.when(end > begin)
                def _():
                    tail = pl.ds(pl.multiple_of(end - BLK, BLK), BLK)
                    cp = pltpu.make_async_copy(zbuf, xs_ref.at[tail], zsem)
                    if phase == 0:
                        cp.start()
                    else:
                        cp.wait()

    base = i * TM

    def row_copy(t, k):
        return pltpu.make_async_copy(h_ref.at[pl.ds(t, 1)], xs_ref.at[pl.ds(dest_ref[k * n_tok + base + t], 1)],
                                     sem)

    def issue(t, carry):
        for k in range(TOP_K):
            row_copy(t, k).start()
        return carry

    lax.fori_loop(0, TM, issue, 0, unroll=8)
    for _ in range(TOP_K):
        pltpu.make_async_copy(h_ref, xs_ref.at[pl.ds(0, TM)], sem).wait()


def _scatter(dest, pend, h2, n_slots):
    n_tok = h2.shape[0]
    return pl.pallas_call(
        _scatter_body,
        grid_spec=pltpu.PrefetchScalarGridSpec(
            num_scalar_prefetch=2,
            grid=(n_tok // TM,),
            in_specs=[pl.BlockSpec((TM, D), lambda i, *_: (i, 0))],
            out_specs=pl.BlockSpec(memory_space=pl.ANY),
            scratch_shapes=[pltpu.VMEM((BLK, D), F32), pltpu.SemaphoreType.DMA, pltpu.SemaphoreType.DMA],
        ),
        out_shape=jax.ShapeDtypeStruct((n_slots, D), F32),
        compiler_params=_params(("arbitrary",)),
        name="moe_scatter",
    )(dest, pend, h2)


W_ROWS = 128


def _expert_body(be_ref, nu_ref, x_ref, wgu_ref, bgu_ref, wd_ref, bd_ref, y_ref, wgu_bf, wd_bf):
    j = pl.program_id(0)
    n_used = nu_ref[0]
    jj = jnp.minimum(j, n_used - 1)
    e = be_ref[jj]
    e_prev = be_ref[jnp.maximum(jj - 1, 0)]

    @pl.when((j == 0) | (e != e_prev))
    def _():
        def cast(r, carry):
            rows = pl.ds(pl.multiple_of(r * W_ROWS, W_ROWS), W_ROWS)
            wgu_bf[rows, :] = wgu_ref[rows, :].astype(BF16)
            wd_bf[rows, :] = wd_ref[rows, :].astype(BF16)
            return carry

        lax.fori_loop(0, D // W_ROWS, cast, 0)

    @pl.when(j < n_used)
    def _():
        hh = jnp.dot(x_ref[...].astype(BF16), wgu_bf[...], preferred_element_type=F32) + bgu_ref[...]
        hg = jnp.minimum(hh[:, :D_FF], SWIGLU_LIMIT)
        hl = jnp.clip(hh[:, D_FF:], -SWIGLU_LIMIT, SWIGLU_LIMIT)
        act = hg * jax.nn.sigmoid(SWIGLU_ALPHA * hg) * (hl + 1.0)
        y_ref[...] = jnp.dot(act.astype(BF16), wd_bf[...], preferred_element_type=F32) + bd_ref[...]

    @pl.when(j >= n_used)
    def _():
        y_ref[...] = jnp.zeros_like(y_ref)


def _experts(block_e, n_used, xs, layer, wgu, bgu, wd, bd):
    n_slots = xs.shape[0]
    n_layers = wgu.shape[0]

    def blk(j, be, nu):
        return (jnp.minimum(j, nu[0] - 1), 0)

    def wmap(j, be, nu):
        return (layer, be[jnp.minimum(j, nu[0] - 1)], 0, 0)

    return pl.pallas_call(
        _expert_body,
        grid_spec=pltpu.PrefetchScalarGridSpec(
            num_scalar_prefetch=2,
            grid=(n_slots // BLK,),
            in_specs=[
                pl.BlockSpec((BLK, D), blk),
                pl.BlockSpec((None, None, D, 2 * D_FF), wmap),
                pl.BlockSpec((None, None, 1, 2 * D_FF), wmap),
                pl.BlockSpec((None, None, D_FF, D), wmap),
                pl.BlockSpec((None, None, 1, D), wmap),
            ],
            out_specs=pl.BlockSpec((BLK, D), lambda j, be, nu: (j, 0)),
            scratch_shapes=[pltpu.VMEM((D, 2 * D_FF), BF16), pltpu.VMEM((D_FF, D), BF16)],
        ),
        out_shape=jax.ShapeDtypeStruct((n_slots, D), F32),
        compiler_params=_params(("arbitrary",)),
        name="moe_experts",
    )(block_e, n_used, xs, wgu, bgu.reshape(n_layers, N_EXPERTS, 1, 2 * D_FF), wd,
      bd.reshape(n_layers, N_EXPERTS, 1, D))


def _combine_body(n_ctx_tiles, dest_ref, x_ref, gate_ref, mod_ref, fnw_ref, y_ref, *rest):
    *o_refs, buf, sem = rest
    i = pl.program_id(0)
    n_tiles = pl.num_programs(0)
    n_tok = n_tiles * TM

    def fetch(tile, slot):
        def issue(t, carry):
            for k in range(TOP_K):
                row = dest_ref[k * n_tok + tile * TM + t]
                pltpu.make_async_copy(y_ref.at[pl.ds(row, 1)], buf.at[slot, k, pl.ds(t, 1)], sem.at[slot]).start()
            return carry

        lax.fori_loop(0, TM, issue, 0, unroll=8)

    @pl.when(i == 0)
    def _():
        fetch(0, 0)

    @pl.when(i + 1 < n_tiles)
    def _():
        fetch(i + 1, (i + 1) % 2)

    slot = i % 2
    for k in range(TOP_K):
        pltpu.make_async_copy(y_ref.at[pl.ds(0, TM)], buf.at[slot, k], sem.at[slot]).wait()
    gate = gate_ref[...].T
    acc = gate[:, 0:1] * buf[slot, 0]
    for k in range(1, TOP_K):
        acc = acc + gate[:, k:k + 1] * buf[slot, k]
    x2 = x_ref[...] + mod_ref[5:6, :] * acc
    if n_ctx_tiles is None:
        o_refs[0][...] = x2
    else:
        ms = jnp.mean(x2 * x2, axis=-1, keepdims=True)
        y = x2 * lax.rsqrt(ms + EPS) * fnw_ref[...]

        @pl.when(i < n_ctx_tiles)
        def _():
            o_refs[0][...] = y

        @pl.when(i >= n_ctx_tiles)
        def _():
            o_refs[1][...] = y


def _combine(dest, x1, gate, mod, fnw, y_slots, n_ctx_tiles, tiles_per_seq, final):
    n_tok = x1.shape[0]

    def grp(i):
        return jnp.where(i < n_ctx_tiles, 0, 1 + (i - n_ctx_tiles) // tiles_per_seq)

    if final:
        out_specs = [pl.BlockSpec((TM, D), lambda i, d: (jnp.minimum(i, n_ctx_tiles - 1), 0)),
                     pl.BlockSpec((TM, D), lambda i, d: (jnp.maximum(i - n_ctx_tiles, 0), 0))]
        out_shape = [jax.ShapeDtypeStruct((n_ctx_tiles * TM, D), F32),
                     jax.ShapeDtypeStruct((n_tok - n_ctx_tiles * TM, D), F32)]
    else:
        out_specs = [pl.BlockSpec((TM, D), lambda i, d: (i, 0))]
        out_shape = [jax.ShapeDtypeStruct((n_tok, D), F32)]
    return pl.pallas_call(
        functools.partial(_combine_body, n_ctx_tiles if final else None),
        grid_spec=pltpu.PrefetchScalarGridSpec(
            num_scalar_prefetch=1,
            grid=(n_tok // TM,),
            in_specs=[
                pl.BlockSpec((TM, D), lambda i, d: (i, 0)),
                pl.BlockSpec((8, TM), lambda i, d: (0, i)),
                pl.BlockSpec((None, 6, D), lambda i, d: (grp(i), 0, 0)),
                pl.BlockSpec((1, D), lambda i, d: (0, 0)),
                pl.BlockSpec(memory_space=pl.ANY),
            ],
            out_specs=out_specs,
            scratch_shapes=[pltpu.VMEM((2, TOP_K, TM, D), F32), pltpu.SemaphoreType.DMA((2,))],
        ),
        out_shape=out_shape,
        compiler_params=_params(("arbitrary",)),
        name="moe_combine",
    )(dest, x1, gate, mod, fnw, y_slots)


def _moe(x1, h2, rwt, rb, mod, fnw, layer, wgu, bgu, wd, bd, n_ctx_tiles, tiles_per_seq, final):
    n_tok = x1.shape[0]
    idx, rank, gate, counts = _route(h2, rwt, rb)
    n_slots = n_tok * TOP_K + N_EXPERTS * BLK
    cnt = counts[:, 0].astype(jnp.int32)
    padded = (cnt + BLK - 1) // BLK * BLK
    pend = jnp.cumsum(padded)
    pstart = pend - padded
    hit = idx[:TOP_K, None, :] == jnp.arange(N_EXPERTS, dtype=jnp.int32)[None, :, None]
    dest = (jnp.sum(jnp.where(hit, pstart[None, :, None], 0), axis=1) + rank[:TOP_K]).reshape(-1).astype(jnp.int32)
    n_blocks = n_slots // BLK
    block_start = jnp.arange(n_blocks, dtype=jnp.int32)[:, None] * BLK
    block_e = jnp.minimum(jnp.sum(pend[None, :] <= block_start, axis=1), N_EXPERTS - 1).astype(jnp.int32)
    n_used = (pend[-1:] // BLK).astype(jnp.int32)
    xs = _scatter(dest, pend.astype(jnp.int32), h2, n_slots)
    ys = _experts(block_e, n_used, xs, layer, wgu, bgu, wd, bd)
    return _combine(dest, x1, gate, mod, fnw, ys, n_ctx_tiles, tiles_per_seq, final)


def _dn_in_body(n_ctx_tiles, x_ref, mod_ref, n1_ref, w_ref, wba_ref, cw_ref, alog_ref, dtb_ref,
                q_ref, k_ref, v_ref, z_ref, bg_ref):
    i = pl.program_id(0)
    row_len = jnp.where(i < n_ctx_tiles, CTX_ROW, GRID_W)
    h = _prenorm(x_ref[...], n1_ref[...], mod_ref[1:2, :], mod_ref[0:1, :]).astype(BF16)
    proj = jnp.dot(h, w_ref[...], preferred_element_type=F32)
    z_ref[...] = proj[:, CONV_CH:].astype(BF16)
    qkv = _silu(_short_conv(proj[:, :CONV_CH], cw_ref, row_len))
    for hd in range(2 * HK):
        s = qkv[:, hd * DK:(hd + 1) * DK]
        n = s * lax.rsqrt(jnp.sum(s * s, axis=-1, keepdims=True) + EPS)
        if hd < HK:
            q_ref[:, hd * DK:(hd + 1) * DK] = n * (DK ** -0.5)
        else:
            k_ref[:, (hd - HK) * DK:(hd - HK + 1) * DK] = n
    v_ref[...] = qkv[:, 2 * QK_DIM:]
    ba = jnp.dot(h, wba_ref[...], preferred_element_type=F32)
    beta = jax.nn.sigmoid(ba)
    a = ba + dtb_ref[...]
    softplus = jnp.maximum(a, 0.0) + jnp.log(1.0 + jnp.exp(-jnp.abs(a)))
    g = -jnp.exp(alog_ref[...]) * softplus
    lane = lax.broadcasted_iota(jnp.int32, (TM, LANES), 1)
    bg_ref[...] = jnp.where(lane < G0, beta, g)


def _dn_in(x, mod, n1, w, wba, cw, alog, dtb, n_ctx_tiles, tiles_per_seq):
    n_tok = x.shape[0]
    tok, const, modm = _tile_maps(n_ctx_tiles, tiles_per_seq)
    return pl.pallas_call(
        functools.partial(_dn_in_body, n_ctx_tiles),
        grid=(n_tok // TM,),
        in_specs=[
            pl.BlockSpec((TM, D), tok),
            pl.BlockSpec((None, 6, D), modm),
            pl.BlockSpec((1, D), const),
            pl.BlockSpec((D, CONV_CH + V_DIM), const),
            pl.BlockSpec((D, LANES), const),
            pl.BlockSpec((3, CONV_CH), const),
            pl.BlockSpec((1, LANES), const),
            pl.BlockSpec((1, LANES), const),
        ],
        out_specs=[
            pl.BlockSpec((TM, QK_DIM), tok),
            pl.BlockSpec((TM, QK_DIM), tok),
            pl.BlockSpec((TM, V_DIM), tok),
            pl.BlockSpec((TM, V_DIM), tok),
            pl.BlockSpec((TM, LANES), tok),
        ],
        out_shape=[
            jax.ShapeDtypeStruct((n_tok, QK_DIM), F32),
            jax.ShapeDtypeStruct((n_tok, QK_DIM), F32),
            jax.ShapeDtypeStruct((n_tok, V_DIM), F32),
            jax.ShapeDtypeStruct((n_tok, V_DIM), BF16),
            jax.ShapeDtypeStruct((n_tok, LANES), F32),
        ],
        compiler_params=_params(("arbitrary",)),
        name="dn_in",
    )(x, mod, n1, w, wba, cw, alog, dtb)


def _hdot(a, b):
    return jnp.dot(a, b, preferred_element_type=F32, precision=HIGHEST)


def _nt_dot(a, b):
    return lax.dot_general(a, b, (((1,), (1,)), ((), ())), preferred_element_type=F32)


def _delta_dir(bwd, q_ref, k_ref, v_ref, bg_ref, o_ref, state):
    r2 = lax.broadcasted_iota(jnp.int32, (CHUNK, LANES), 0)
    lane = lax.broadcasted_iota(jnp.int32, (CHUNK, LANES), 1)
    c2 = lane & (CHUNK - 1)
    left = lane < CHUNK
    causal2 = (r2 <= c2) if bwd else (r2 >= c2)
    strict2 = (r2 < c2) if bwd else (r2 > c2)
    eye2 = (r2 == c2).astype(F32)
    lvl = r2 ^ c2
    b0 = HV if bwd else 0
    g0 = 2 * HV + b0
    bg = bg_ref[...]
    gc = _hdot(causal2[:, :CHUNK].astype(F32), bg)
    gct = gc.T
    gtot = jnp.sum(bg, axis=0, keepdims=True)
    egc = jnp.exp(gc)
    eend = jnp.exp(gtot - gc)
    etot = jnp.exp(gtot)

    def pair_cols(a, j):
        return jnp.where(left, a[:, j:j + 1], a[:, j + 1:j + 2])

    def bdiag(x):
        z = jnp.zeros_like(x)
        return jnp.concatenate([jnp.where(left, x, z), jnp.where(left, z, x)], axis=0).astype(BF16)

    eye_k = (lax.broadcasted_iota(jnp.int32, (DK, DK), 0) == lax.broadcasted_iota(jnp.int32, (DK, DK), 1)).astype(BF16)
    ns, attns, kts = [], [], []
    for p in range(HK):
        j = g0 + PAIR * p
        k16 = k_ref[:, p * DK:(p + 1) * DK].astype(BF16)
        q16 = q_ref[:, p * DK:(p + 1) * DK].astype(BF16)
        kq = _nt_dot(jnp.concatenate([k16, q16], axis=0), jnp.concatenate([k16, k16], axis=0))
        grow2 = jnp.concatenate([gct[j:j + 1, :], gct[j + 1:j + 2, :]], axis=1)
        decay2 = jnp.exp(jnp.where(causal2, pair_cols(gc, j) - grow2, NEG))
        ns.append(jnp.where(strict2, kq[:CHUNK] * decay2 * pair_cols(bg, b0 + PAIR * p), 0.0))
        attns.append(kq[CHUNK:] * decay2)
        kts.append(_nt_dot(eye_k, k16).astype(BF16))
        yield

    def mdot(a, b):
        return jnp.dot(a.astype(BF16), bdiag(b), preferred_element_type=F32)

    n4 = [jnp.where(lvl < 4, n, 0.0) for n in ns]
    sq = [mdot(a, a) for a in n4]
    yield
    ts = [eye2 - a for a in n4]
    ts = [t + mdot(t, s) for t, s in zip(ts, sq)]
    yield
    for bit in range(2, 6):
        ys = [mdot(t, jnp.where((lvl >> bit) == 1, n, 0.0)) for t, n in zip(ts, ns)]
        yield
        ts = [t - mdot(y, t) for t, y in zip(ts, ys)]
        yield

    sols = []
    for p in range(HK):
        kh = k_ref[:, p * DK:(p + 1) * DK]
        rhs = []
        for e in range(PAIR):
            h = PAIR * p + e
            beta = bg[:, b0 + h:b0 + h + 1]
            rhs.append(jnp.concatenate(
                [v_ref[:, h * DV:(h + 1) * DV] * beta, kh * (beta * egc[:, g0 + h:g0 + h + 1])], axis=1))
        sols.append(jnp.dot(bdiag(ts[p]), jnp.concatenate(rhs, axis=0).astype(BF16), preferred_element_type=F32))
        yield

    outs = []
    for h in range(HV):
        p, e = divmod(h, PAIR)
        qg = q_ref[:, p * DK:(p + 1) * DK] * egc[:, g0 + h:g0 + h + 1]
        lhs = jnp.concatenate([sols[p][e * CHUNK:(e + 1) * CHUNK, DV:], qg], axis=0)
        outs.append(_bdot(lhs, state[bwd, h]))
        yield

    for p in range(HK):
        h0 = PAIR * p
        vns = [sols[p][e * CHUNK:(e + 1) * CHUNK, :DV] - outs[h0 + e][:CHUNK] for e in range(PAIR)]
        intra = jnp.dot(bdiag(attns[p]), jnp.concatenate(vns, axis=0).astype(BF16), preferred_element_type=F32)
        for e in range(PAIR):
            o_ref[:, (h0 + e) * DV:(h0 + e + 1) * DV] = (outs[h0 + e][CHUNK:] + intra[e * CHUNK:(e + 1) * CHUNK]).astype(BF16)
        scaled = jnp.concatenate([vns[e] * eend[:, g0 + h0 + e:g0 + h0 + e + 1] for e in range(PAIR)], axis=1)
        kv = jnp.dot(kts[p], scaled.astype(BF16), preferred_element_type=F32)
        for e in range(PAIR):
            h = h0 + e
            state[bwd, h] = state[bwd, h] * etot[:, g0 + h:g0 + h + 1] + kv[:, e * DV:(e + 1) * DV]
        yield


FIRST, LAST, DEC = 1, 2, 4


def _delta_body(fb_ref, bb_ref, si_ref, oi_ref, fl_ref, qf_ref, kf_ref, vf_ref, bgf_ref, qb_ref, kb_ref, vb_ref,
                bgb_ref, s0_ref, of_ref, ob_ref, sout_ref, state):
    flags = fl_ref[pl.program_id(0)]

    @pl.when((flags & FIRST) != 0)
    def _():
        state[...] = jnp.where((flags & DEC) != 0, s0_ref[...], 0.0)

    live = [_delta_dir(0, qf_ref, kf_ref, vf_ref, bgf_ref, of_ref, state),
            _delta_dir(1, qb_ref, kb_ref, vb_ref, bgb_ref, ob_ref, state)]
    while live:
        live = [g for g in live if next(g, StopIteration) is not StopIteration]

    @pl.when((flags & (LAST | DEC)) == LAST)
    def _():
        sout_ref[...] = state[...]


def _delta(q, k, v, bg, state_delta, n_ctx, ctx_chunks, n_dec, dec_chunks):
    n_tok = q.shape[0]
    fb, bb, si, oi, fl = [], [], [], [], []
    for dec, n_seq, n, base in ((0, n_ctx, ctx_chunks, 0), (1, n_dec, dec_chunks, n_ctx * ctx_chunks)):
        for b in range(n_seq):
            for c in range(n):
                fb.append(base + b * n + c)
                bb.append(base + b * n + n - 1 - c)
                si.append(b if dec else 0)
                oi.append(n_ctx - 1 if dec else b)
                fl.append((FIRST if c == 0 else 0) | (LAST if c == n - 1 else 0) | (DEC if dec else 0))
    tables = [jnp.asarray(t, jnp.int32) for t in (fb, bb, si, oi, fl)]
    fwd = lambda s, fb, bb, si, oi, fl: (fb[s], 0)
    bwd = lambda s, fb, bb, si, oi, fl: (bb[s], 0)
    tok_specs = lambda m: [pl.BlockSpec((CHUNK, QK_DIM), m), pl.BlockSpec((CHUNK, QK_DIM), m),
                           pl.BlockSpec((CHUNK, V_DIM), m), pl.BlockSpec((CHUNK, LANES), m)]
    return pl.pallas_call(
        _delta_body,
        grid_spec=pltpu.PrefetchScalarGridSpec(
            num_scalar_prefetch=5,
            grid=(len(fb),),
            in_specs=tok_specs(fwd) + tok_specs(bwd) + [
                pl.BlockSpec((None, None, 2, HV, DK, DV), lambda s, fb, bb, si, oi, fl: (si[s], 0, 0, 0, 0, 0)),
            ],
            out_specs=[
                pl.BlockSpec((CHUNK, V_DIM), fwd),
                pl.BlockSpec((CHUNK, V_DIM), bwd),
                pl.BlockSpec((None, 2, HV, DK, DV), lambda s, fb, bb, si, oi, fl: (oi[s], 0, 0, 0, 0)),
            ],
            scratch_shapes=[pltpu.VMEM((2, HV, DK, DV), F32)],
        ),
        out_shape=[
            jax.ShapeDtypeStruct((n_tok, V_DIM), BF16),
            jax.ShapeDtypeStruct((n_tok, V_DIM), BF16),
            jax.ShapeDtypeStruct((n_ctx, 2, HV, DK, DV), F32),
        ],
        compiler_params=_params(("arbitrary",)),
        name="delta_rule",
    )(*tables, q, k, v, bg, q, k, v, bg, state_delta)


def _dn_out_body(of_ref, ob_ref, z_ref, x_ref, mod_ref, nw_ref, wout_ref, n2_ref, x1_ref, h2_ref):
    o = of_ref[...].astype(F32) + ob_ref[...].astype(F32)
    z = z_ref[...].astype(F32)
    parts = []
    for h in range(HV):
        oh = o[:, h * DV:(h + 1) * DV]
        nh = oh * lax.rsqrt(jnp.mean(oh * oh, axis=-1, keepdims=True) + EPS) * nw_ref[...]
        parts.append((nh * _silu(z[:, h * DV:(h + 1) * DV])).astype(BF16))
    mix = jnp.dot(jnp.concatenate(parts, axis=1), wout_ref[...], preferred_element_type=F32)
    _moe_input(x_ref[...] + mod_ref[2:3, :] * mix, mod_ref, n2_ref, x1_ref, h2_ref)


def _dn_out(o_f, o_b, z, x, mod, nw, wout, n2, n_ctx_tiles, tiles_per_seq):
    n_tok = x.shape[0]
    tok, const, modm = _tile_maps(n_ctx_tiles, tiles_per_seq)
    out_specs, out_shapes = _mixer_outs(n_tok, tok)
    return pl.pallas_call(
        _dn_out_body,
        grid=(n_tok // TM,),
        in_specs=[
            pl.BlockSpec((TM, V_DIM), tok),
            pl.BlockSpec((TM, V_DIM), tok),
            pl.BlockSpec((TM, V_DIM), tok),
            pl.BlockSpec((TM, D), tok),
            pl.BlockSpec((None, 6, D), modm),
            pl.BlockSpec((1, DV), const),
            pl.BlockSpec((V_DIM, D), const),
            pl.BlockSpec((1, D), const),
        ],
        out_specs=out_specs,
        out_shape=out_shapes,
        compiler_params=_params(("arbitrary",)),
        name="dn_out",
    )(o_f, o_b, z, x, mod, nw, wout, n2)


def kernel(x_prompt, x_sample, state_delta, c, c_ctx, ada_w, ada_b, norm1_w, norm2_w, conv_in_w, conv_w,
           conv_out_w, dn_in_w, dn_conv_w, dn_a_log, dn_dt_bias, dn_norm_w, dn_out_w, router_w, router_b,
           exp_gu_w, exp_gu_b, exp_down_w, exp_down_b, final_norm_w):
    n_ctx, ctx_len, _ = x_prompt.shape
    n_dec, dec_len, _ = x_sample.shape
    assert ctx_len == CTX_ROW and dec_len % TM == 0 and n_dec + 1 <= N_COND
    n_ctx_tok = n_ctx * ctx_len
    n_ctx_tiles = n_ctx_tok // TM
    tiles_per_seq = dec_len // TM

    cond = jnp.zeros((N_COND, D), F32).at[0].set(c_ctx).at[1:1 + n_dec].set(c)
    mod = _ada(cond, ada_w, ada_b)

    def moe(l, x1, h2, final):
        return _moe(x1, h2, router_w[l].T, router_b[l][:, None], mod[l], final_norm_w[None, :], l, exp_gu_w,
                    exp_gu_b, exp_down_w, exp_down_b, n_ctx_tiles, tiles_per_seq, final)

    x1, h2 = _conv_layer(x_prompt.reshape(n_ctx_tok, D), x_sample.reshape(n_dec * dec_len, D), mod[0],
                         norm1_w[0][None, :], conv_in_w[0].astype(BF16), conv_w[0], conv_out_w[0].astype(BF16),
                         norm2_w[0][None, :], n_ctx_tiles, tiles_per_seq)
    x, = moe(0, x1, h2, False)

    w_in = dn_in_w[0]
    w_ba = jnp.zeros((D, LANES), F32).at[:, :4 * HV].set(w_in[:, CONV_CH + V_DIM:]).astype(BF16)
    lane_pad = lambda a: jnp.zeros((1, LANES), F32).at[0, 2 * HV:4 * HV].set(a.reshape(-1))
    q, k, v, z, bg = _dn_in(x, mod[1], norm1_w[1][None, :], w_in[:, :CONV_CH + V_DIM].astype(BF16), w_ba,
                            dn_conv_w[0], lane_pad(dn_a_log[0]), lane_pad(dn_dt_bias[0]), n_ctx_tiles,
                            tiles_per_seq)
    o_f, o_b, s_ctx = _delta(q, k, v, bg, state_delta, n_ctx, ctx_len // CHUNK, n_dec, dec_len // CHUNK)
    x1, h2 = _dn_out(o_f, o_b, z, x, mod[1], dn_norm_w[0][None, :], dn_out_w[0].astype(BF16),
                     norm2_w[1][None, :], n_ctx_tiles, tiles_per_seq)
    y_ctx, y_dec = moe(1, x1, h2, True)
    return y_ctx.reshape(n_ctx, ctx_len, D), y_dec.reshape(n_dec, dec_len, D), s_ctx[:, None]
```

```python
import functools

import jax
import jax.numpy as jnp
from jax import lax
from jax.experimental import pallas as pl
from jax.experimental.pallas import tpu as pltpu

F32 = jnp.float32
BF16 = jnp.bfloat16

D = 1024
TM = 256
CTX_ROW = 256
GRID_W = 64
HK, HV, DK, DV = 8, 16, 128, 128
QK_DIM = HK * DK
V_DIM = HV * DV
CONV_CH = 2 * QK_DIM + V_DIM
CHUNK = 64
N_EXPERTS = 32
TOP_K = 4
D_FF = 1024
DP = D // 2
SWIGLU_ALPHA = 1.702
SWIGLU_LIMIT = 7.0
EPS = 1e-6
LANES = 128
N_COND = 8
BLK = 512
NEG = -3.0e38
VMEM_LIMIT = 56 * 1024 * 1024
HIGHEST = lax.Precision.HIGHEST
G0 = 2 * HV
PAIR = HV // HK


def _silu(x):
    return x * jax.nn.sigmoid(x)


def _bdot(a, b):
    return jnp.dot(a.astype(BF16), b.astype(BF16), preferred_element_type=F32)


def _pack_bf16(x):
    m = x.shape[1] // 2

    def rne(v):
        bits = lax.bitcast_convert_type(v, jnp.uint32)
        return (bits + jnp.uint32(0x7FFF) + ((bits >> 16) & jnp.uint32(1))) >> 16

    return rne(x[:, :m]) | (rne(x[:, m:]) << 16)


def _unpack_bf16(p):
    lo = lax.bitcast_convert_type(p << 16, F32)
    hi = lax.bitcast_convert_type(p & jnp.uint32(0xFFFF0000), F32)
    return jnp.concatenate([lo.astype(BF16), hi.astype(BF16)], axis=1)


def _params(sem):
    return pltpu.CompilerParams(dimension_semantics=sem, vmem_limit_bytes=VMEM_LIMIT)


def _ada_body(c_ref, w_ref, b_ref, o_ref):
    o_ref[...] = _bdot(_silu(c_ref[...]), w_ref[...]) + b_ref[...]


def _ada(cond, ada_w, ada_b):
    n_layers = ada_w.shape[0]
    tn = 1024
    out = pl.pallas_call(
        _ada_body,
        grid=(n_layers, 6 * D // tn),
        in_specs=[
            pl.BlockSpec((N_COND, D), lambda l, j: (0, 0)),
            pl.BlockSpec((None, D, tn), lambda l, j: (l, 0, j)),
            pl.BlockSpec((None, 1, tn), lambda l, j: (l, 0, j)),
        ],
        out_specs=pl.BlockSpec((None, N_COND, tn), lambda l, j: (l, 0, j)),
        out_shape=jax.ShapeDtypeStruct((n_layers, N_COND, 6 * D), F32),
        compiler_params=_params(("arbitrary", "arbitrary")),
        name="ada",
    )(cond, ada_w, ada_b.reshape(n_layers, 1, 6 * D))
    return out.reshape(n_layers, N_COND, 6, D)


def _prenorm(x, nw, sc, sh):
    ms = jnp.mean(x * x, axis=-1, keepdims=True)
    return (x * lax.rsqrt(ms + EPS) * nw) * (1.0 + sc) + sh


def _short_conv(v, w_ref, row_len):
    n = v.shape[0]
    pos = lax.broadcasted_iota(jnp.int32, (n, 1), 0) & (row_len - 1)
    prev = jnp.where(pos == 0, 0.0, pltpu.roll(v, 1, 0))
    nxt = jnp.where(pos == row_len - 1, 0.0, pltpu.roll(v, n - 1, 0))
    return prev * w_ref[0:1, :] + v * w_ref[1:2, :] + nxt * w_ref[2:3, :]


RT = 1024


def _route_body(h_ref, rwt_ref, rb_ref, idx_ref, rank_ref, gate_ref, cnt_ref, carry_ref):
    @pl.when(pl.program_id(0) == 0)
    def _():
        carry_ref[...] = jnp.zeros_like(carry_ref)

    logits = lax.dot_general(rwt_ref[...], h_ref[...], (((1,), (1,)), ((), ())), preferred_element_type=F32,
                             precision=HIGHEST) + rb_ref[...]
    eid = lax.broadcasted_iota(jnp.int32, (N_EXPERTS, RT), 0)
    work = logits
    sel = jnp.zeros((N_EXPERTS, RT), F32)
    vals, ids, hots = [], [], []
    for _ in range(TOP_K):
        m = jnp.max(work, axis=0, keepdims=True)
        ik = jnp.min(jnp.where(work == m, eid, N_EXPERTS), axis=0, keepdims=True)
        hot = eid == ik
        work = jnp.where(hot, NEG, work)
        sel = sel + hot.astype(F32)
        vals.append(m)
        ids.append(ik)
        hots.append(hot)
    es = [jnp.exp(v - vals[0]) for v in vals]
    den = es[0] + es[1] + es[2] + es[3]
    earlier = (lax.broadcasted_iota(jnp.int32, (TM, TM), 0) < lax.broadcasted_iota(jnp.int32, (TM, TM), 1))
    earlier = earlier.astype(BF16)
    running = carry_ref[:, 0:1]
    ranks = []
    for j in range(RT // TM):
        part = sel[:, j * TM:(j + 1) * TM]
        ranks.append(jnp.dot(part.astype(BF16), earlier, preferred_element_type=F32) + running)
        running = running + jnp.sum(part, axis=1, keepdims=True)
    rank_all = jnp.concatenate(ranks, axis=1)
    slot = lax.broadcasted_iota(jnp.int32, (8, RT), 0)
    idx_o = jnp.zeros((8, RT), jnp.int32)
    rank_o = jnp.zeros((8, RT), jnp.int32)
    gate_o = jnp.zeros((8, RT), F32)
    for k in range(TOP_K):
        rk = jnp.sum(jnp.where(hots[k], rank_all, 0.0), axis=0, keepdims=True)
        idx_o = jnp.where(slot == k, ids[k], idx_o)
        rank_o = jnp.where(slot == k, rk.astype(jnp.int32), rank_o)
        gate_o = jnp.where(slot == k, es[k] / den, gate_o)
    idx_ref[...] = idx_o
    rank_ref[...] = rank_o
    gate_ref[...] = gate_o
    total = jnp.broadcast_to(running, (N_EXPERTS, LANES))
    carry_ref[...] = total
    cnt_ref[...] = total


def _route(h2, rwt, rb):
    n_tok = h2.shape[0]
    const = lambda i: (0, 0)
    cols = lambda i: (0, i)
    return pl.pallas_call(
        _route_body,
        grid=(n_tok // RT,),
        in_specs=[
            pl.BlockSpec((RT, D), lambda i: (i, 0)),
            pl.BlockSpec((N_EXPERTS, D), const),
            pl.BlockSpec((N_EXPERTS, 1), const),
        ],
        out_specs=[
            pl.BlockSpec((8, RT), cols),
            pl.BlockSpec((8, RT), cols),
            pl.BlockSpec((8, RT), cols),
            pl.BlockSpec((N_EXPERTS, LANES), const),
        ],
        out_shape=[
            jax.ShapeDtypeStruct((8, n_tok), jnp.int32),
            jax.ShapeDtypeStruct((8, n_tok), jnp.int32),
            jax.ShapeDtypeStruct((8, n_tok), F32),
            jax.ShapeDtypeStruct((N_EXPERTS, LANES), F32),
        ],
        scratch_shapes=[pltpu.VMEM((N_EXPERTS, LANES), F32)],
        compiler_params=_params(("arbitrary",)),
        name="moe_route",
    )(h2, rwt, rb)


def _tile_maps(n_ctx_tiles, tiles_per_seq):
    def grp(i):
        return jnp.where(i < n_ctx_tiles, 0, 1 + (i - n_ctx_tiles) // tiles_per_seq)

    tok = lambda i: (i, 0)
    const = lambda i: (0, 0)
    mod = lambda i: (grp(i), 0, 0)
    return tok, const, mod


def _mixer_outs(n_tok, tok):
    return ([pl.BlockSpec((TM, D), tok), pl.BlockSpec((TM, D), tok), pl.BlockSpec((TM, DP), tok)],
            [jax.ShapeDtypeStruct((n_tok, D), F32), jax.ShapeDtypeStruct((n_tok, D), F32),
             jax.ShapeDtypeStruct((n_tok, DP), jnp.uint32)])


def _moe_input(x1, mod_ref, n2_ref, x1_ref, h2_ref, h2p_ref):
    x1_ref[...] = x1
    h2 = _prenorm(x1, n2_ref[...], mod_ref[4:5, :], mod_ref[3:4, :])
    h2_ref[...] = h2
    h2p_ref[...] = _pack_bf16(h2)


def _conv_body(n_ctx_tiles, xp_ref, xs_ref, mod_ref, n1_ref, win_ref, cw_ref, wout_ref, n2_ref, x1_ref, h2_ref,
               h2p_ref):
    i = pl.program_id(0)
    row_len = jnp.where(i < n_ctx_tiles, CTX_ROW, GRID_W)
    x = jnp.where(i < n_ctx_tiles, xp_ref[...], xs_ref[...])
    h = _prenorm(x, n1_ref[...], mod_ref[1:2, :], mod_ref[0:1, :])
    proj = _bdot(h, win_ref[...])
    b, cg, u = proj[:, :D], proj[:, D:2 * D], proj[:, 2 * D:]
    mix = _bdot(b * _short_conv(cg * u, cw_ref, row_len), wout_ref[...])
    _moe_input(x + mod_ref[2:3, :] * mix, mod_ref, n2_ref, x1_ref, h2_ref, h2p_ref)


def _conv_layer(x_ctx, x_dec, mod, n1, win, cw, wout, n2, n_ctx_tiles, tiles_per_seq):
    n_tok = x_ctx.shape[0] + x_dec.shape[0]
    tok, const, modm = _tile_maps(n_ctx_tiles, tiles_per_seq)
    out_specs, out_shapes = _mixer_outs(n_tok, tok)
    return pl.pallas_call(
        functools.partial(_conv_body, n_ctx_tiles),
        grid=(n_tok // TM,),
        in_specs=[
            pl.BlockSpec((TM, D), lambda i: (jnp.minimum(i, n_ctx_tiles - 1), 0)),
            pl.BlockSpec((TM, D), lambda i: (jnp.maximum(i - n_ctx_tiles, 0), 0)),
            pl.BlockSpec((None, 6, D), modm),
            pl.BlockSpec((1, D), const),
            pl.BlockSpec((D, 3 * D), const),
            pl.BlockSpec((3, D), const),
            pl.BlockSpec((D, D), const),
            pl.BlockSpec((1, D), const),
        ],
        out_specs=out_specs,
        out_shape=out_shapes,
        compiler_params=_params(("arbitrary",)),
        name="conv_layer",
    )(x_ctx, x_dec, mod, n1, win, cw, wout, n2)


def _scatter_body(dest_ref, pend_ref, h_ref, xs_ref, zbuf, sem, zsem):
    i = pl.program_id(0)
    n_tok = pl.num_programs(0) * TM

    @pl.when(i == 0)
    def _():
        zbuf[...] = jnp.zeros_like(zbuf)

        def zero_block(j):
            return pltpu.make_async_copy(zbuf, xs_ref.at[pl.ds(pl.multiple_of(j * BLK, BLK), BLK)], zsem)

        def start(j, carry):
            zero_block(j).start()
            return carry

        def wait(j, carry):
            zero_block(j).wait()
            return carry

        n_used = pend_ref[N_EXPERTS - 1] // BLK
        lax.fori_loop(n_used, xs_ref.shape[0] // BLK, start, 0)
        lax.fori_loop(n_used, xs_ref.shape[0] // BLK, wait, 0)
        for phase in range(2):
            for e in range(N_EXPERTS):
                end = pend_ref[e]
                begin = pend_ref[e - 1] if e else 0

                @pl.when(end > begin)
                def _():
                    tail = pl.ds(pl.multiple_of(end - BLK, BLK), BLK)
                    cp = pltpu.make_async_copy(zbuf, xs_ref.at[tail], zsem)
                    if phase == 0:
                        cp.start()
                    else:
                        cp.wait()

    base = i * TM

    def row_copy(t, k):
        return pltpu.make_async_copy(h_ref.at[pl.ds(t, 1)], xs_ref.at[pl.ds(dest_ref[k * n_tok + base + t], 1)],
                                     sem)

    def issue(t, carry):
        for k in range(TOP_K):
            row_copy(t, k).start()
        return carry

    lax.fori_loop(0, TM, issue, 0, unroll=8)
    for _ in range(TOP_K):
        pltpu.make_async_copy(h_ref, xs_ref.at[pl.ds(0, TM)], sem).wait()


def _scatter(dest, pend, h2p, n_slots):
    n_tok = h2p.shape[0]
    return pl.pallas_call(
        _scatter_body,
        grid_spec=pltpu.PrefetchScalarGridSpec(
            num_scalar_prefetch=2,
            grid=(n_tok // TM,),
            in_specs=[pl.BlockSpec((TM, DP), lambda i, *_: (i, 0))],
            out_specs=pl.BlockSpec(memory_space=pl.ANY),
            scratch_shapes=[pltpu.VMEM((BLK, DP), jnp.uint32), pltpu.SemaphoreType.DMA, pltpu.SemaphoreType.DMA],
        ),
        out_shape=jax.ShapeDtypeStruct((n_slots, DP), jnp.uint32),
        compiler_params=_params(("arbitrary",)),
        name="moe_scatter",
    )(dest, pend, h2p)


W_ROWS = 128


def _expert_body(be_ref, nu_ref, x_ref, wgu_ref, bgu_ref, wd_ref, bd_ref, y_ref, wgu_bf, wd_bf):
    j = pl.program_id(0)
    n_used = nu_ref[0]
    jj = jnp.minimum(j, n_used - 1)
    e = be_ref[jj]
    e_prev = be_ref[jnp.maximum(jj - 1, 0)]

    @pl.when((j == 0) | (e != e_prev))
    def _():
        def cast(r, carry):
            rows = pl.ds(pl.multiple_of(r * W_ROWS, W_ROWS), W_ROWS)
            wgu_bf[rows, :] = wgu_ref[rows, :].astype(BF16)
            wd_bf[rows, :] = wd_ref[rows, :].astype(BF16)
            return carry

        lax.fori_loop(0, D // W_ROWS, cast, 0)

    @pl.when(j < n_used)
    def _():
        hh = jnp.dot(_unpack_bf16(x_ref[...]), wgu_bf[...], preferred_element_type=F32) + bgu_ref[...]
        hg = jnp.minimum(hh[:, :D_FF], SWIGLU_LIMIT)
        hl = jnp.clip(hh[:, D_FF:], -SWIGLU_LIMIT, SWIGLU_LIMIT)
        act = hg * jax.nn.sigmoid(SWIGLU_ALPHA * hg) * (hl + 1.0)
        y_ref[...] = _pack_bf16(jnp.dot(act.astype(BF16), wd_bf[...], preferred_element_type=F32) + bd_ref[...])

    @pl.when(j >= n_used)
    def _():
        y_ref[...] = jnp.zeros_like(y_ref)


def _experts(block_e, n_used, xs, layer, wgu, bgu, wd, bd):
    n_slots = xs.shape[0]
    n_layers = wgu.shape[0]

    def blk(j, be, nu):
        return (jnp.minimum(j, nu[0] - 1), 0)

    def wmap(j, be, nu):
        return (layer, be[jnp.minimum(j, nu[0] - 1)], 0, 0)

    return pl.pallas_call(
        _expert_body,
        grid_spec=pltpu.PrefetchScalarGridSpec(
            num_scalar_prefetch=2,
            grid=(n_slots // BLK,),
            in_specs=[
                pl.BlockSpec((BLK, DP), blk),
                pl.BlockSpec((None, None, D, 2 * D_FF), wmap),
                pl.BlockSpec((None, None, 1, 2 * D_FF), wmap),
                pl.BlockSpec((None, None, D_FF, D), wmap),
                pl.BlockSpec((None, None, 1, D), wmap),
            ],
            out_specs=pl.BlockSpec((BLK, DP), lambda j, be, nu: (j, 0)),
            scratch_shapes=[pltpu.VMEM((D, 2 * D_FF), BF16), pltpu.VMEM((D_FF, D), BF16)],
        ),
        out_shape=jax.ShapeDtypeStruct((n_slots, DP), jnp.uint32),
        compiler_params=_params(("arbitrary",)),
        name="moe_experts",
    )(block_e, n_used, xs, wgu, bgu.reshape(n_layers, N_EXPERTS, 1, 2 * D_FF), wd,
      bd.reshape(n_layers, N_EXPERTS, 1, D))


def _combine_body(n_ctx_tiles, dest_ref, x_ref, gate_ref, mod_ref, fnw_ref, y_ref, *rest):
    *o_refs, buf, sem = rest
    i = pl.program_id(0)
    n_tiles = pl.num_programs(0)
    n_tok = n_tiles * TM

    def fetch(tile, slot):
        def issue(t, carry):
            for k in range(TOP_K):
                row = dest_ref[k * n_tok + tile * TM + t]
                pltpu.make_async_copy(y_ref.at[pl.ds(row, 1)], buf.at[slot, k, pl.ds(t, 1)], sem.at[slot]).start()
            return carry

        lax.fori_loop(0, TM, issue, 0, unroll=8)

    @pl.when(i == 0)
    def _():
        fetch(0, 0)

    @pl.when(i + 1 < n_tiles)
    def _():
        fetch(i + 1, (i + 1) % 2)

    slot = i % 2
    for k in range(TOP_K):
        pltpu.make_async_copy(y_ref.at[pl.ds(0, TM)], buf.at[slot, k], sem.at[slot]).wait()
    gate = gate_ref[...].T
    acc = gate[:, 0:1] * _unpack_bf16(buf[slot, 0]).astype(F32)
    for k in range(1, TOP_K):
        acc = acc + gate[:, k:k + 1] * _unpack_bf16(buf[slot, k]).astype(F32)
    x2 = x_ref[...] + mod_ref[5:6, :] * acc
    if n_ctx_tiles is None:
        o_refs[0][...] = x2
    else:
        ms = jnp.mean(x2 * x2, axis=-1, keepdims=True)
        y = x2 * lax.rsqrt(ms + EPS) * fnw_ref[...]

        @pl.when(i < n_ctx_tiles)
        def _():
            o_refs[0][...] = y

        @pl.when(i >= n_ctx_tiles)
        def _():
            o_refs[1][...] = y


def _combine(dest, x1, gate, mod, fnw, y_slots, n_ctx_tiles, tiles_per_seq, final):
    n_tok = x1.shape[0]

    def grp(i):
        return jnp.where(i < n_ctx_tiles, 0, 1 + (i - n_ctx_tiles) // tiles_per_seq)

    if final:
        out_specs = [pl.BlockSpec((TM, D), lambda i, d: (jnp.minimum(i, n_ctx_tiles - 1), 0)),
                     pl.BlockSpec((TM, D), lambda i, d: (jnp.maximum(i - n_ctx_tiles, 0), 0))]
        out_shape = [jax.ShapeDtypeStruct((n_ctx_tiles * TM, D), F32),
                     jax.ShapeDtypeStruct((n_tok - n_ctx_tiles * TM, D), F32)]
    else:
        out_specs = [pl.BlockSpec((TM, D), lambda i, d: (i, 0))]
        out_shape = [jax.ShapeDtypeStruct((n_tok, D), F32)]
    return pl.pallas_call(
        functools.partial(_combine_body, n_ctx_tiles if final else None),
        grid_spec=pltpu.PrefetchScalarGridSpec(
            num_scalar_prefetch=1,
            grid=(n_tok // TM,),
            in_specs=[
                pl.BlockSpec((TM, D), lambda i, d: (i, 0)),
                pl.BlockSpec((8, TM), lambda i, d: (0, i)),
                pl.BlockSpec((None, 6, D), lambda i, d: (grp(i), 0, 0)),
                pl.BlockSpec((1, D), lambda i, d: (0, 0)),
                pl.BlockSpec(memory_space=pl.ANY),
            ],
            out_specs=out_specs,
            scratch_shapes=[pltpu.VMEM((2, TOP_K, TM, DP), jnp.uint32), pltpu.SemaphoreType.DMA((2,))],
        ),
        out_shape=out_shape,
        compiler_params=_params(("arbitrary",)),
        name="moe_combine",
    )(dest, x1, gate, mod, fnw, y_slots)


def _moe(x1, h2, h2p, rwt, rb, mod, fnw, layer, wgu, bgu, wd, bd, n_ctx_tiles, tiles_per_seq, final):
    n_tok = x1.shape[0]
    idx, rank, gate, counts = _route(h2, rwt, rb)
    n_slots = n_tok * TOP_K + N_EXPERTS * BLK
    cnt = counts[:, 0].astype(jnp.int32)
    padded = (cnt + BLK - 1) // BLK * BLK
    pend = jnp.cumsum(padded)
    pstart = pend - padded
    hit = idx[:TOP_K, None, :] == jnp.arange(N_EXPERTS, dtype=jnp.int32)[None, :, None]
    dest = (jnp.sum(jnp.where(hit, pstart[None, :, None], 0), axis=1) + rank[:TOP_K]).reshape(-1).astype(jnp.int32)
    n_blocks = n_slots // BLK
    block_start = jnp.arange(n_blocks, dtype=jnp.int32)[:, None] * BLK
    block_e = jnp.minimum(jnp.sum(pend[None, :] <= block_start, axis=1), N_EXPERTS - 1).astype(jnp.int32)
    n_used = (pend[-1:] // BLK).astype(jnp.int32)
    xs = _scatter(dest, pend.astype(jnp.int32), h2p, n_slots)
    ys = _experts(block_e, n_used, xs, layer, wgu, bgu, wd, bd)
    return _combine(dest, x1, gate, mod, fnw, ys, n_ctx_tiles, tiles_per_seq, final)


def _dn_in_body(n_ctx_tiles, x_ref, mod_ref, n1_ref, w_ref, wba_ref, cw_ref, alog_ref, dtb_ref,
                q_ref, k_ref, v_ref, z_ref, bg_ref):
    i = pl.program_id(0)
    row_len = jnp.where(i < n_ctx_tiles, CTX_ROW, GRID_W)
    h = _prenorm(x_ref[...], n1_ref[...], mod_ref[1:2, :], mod_ref[0:1, :]).astype(BF16)
    proj = jnp.dot(h, w_ref[...], preferred_element_type=F32)
    z_ref[...] = proj[:, CONV_CH:].astype(BF16)
    qkv = _silu(_short_conv(proj[:, :CONV_CH], cw_ref, row_len))
    for hd in range(2 * HK):
        s = qkv[:, hd * DK:(hd + 1) * DK]
        n = s * lax.rsqrt(jnp.sum(s * s, axis=-1, keepdims=True) + EPS)
        if hd < HK:
            q_ref[:, hd * DK:(hd + 1) * DK] = n * (DK ** -0.5)
        else:
            k_ref[:, (hd - HK) * DK:(hd - HK + 1) * DK] = n
    v_ref[...] = qkv[:, 2 * QK_DIM:]
    ba = jnp.dot(h, wba_ref[...], preferred_element_type=F32)
    beta = jax.nn.sigmoid(ba)
    a = ba + dtb_ref[...]
    softplus = jnp.maximum(a, 0.0) + jnp.log(1.0 + jnp.exp(-jnp.abs(a)))
    g = -jnp.exp(alog_ref[...]) * softplus
    lane = lax.broadcasted_iota(jnp.int32, (TM, LANES), 1)
    bg_ref[...] = jnp.where(lane < G0, beta, g)


def _dn_in(x, mod, n1, w, wba, cw, alog, dtb, n_ctx_tiles, tiles_per_seq):
    n_tok = x.shape[0]
    tok, const, modm = _tile_maps(n_ctx_tiles, tiles_per_seq)
    return pl.pallas_call(
        functools.partial(_dn_in_body, n_ctx_tiles),
        grid=(n_tok // TM,),
        in_specs=[
            pl.BlockSpec((TM, D), tok),
            pl.BlockSpec((None, 6, D), modm),
            pl.BlockSpec((1, D), const),
            pl.BlockSpec((D, CONV_CH + V_DIM), const),
            pl.BlockSpec((D, LANES), const),
            pl.BlockSpec((3, CONV_CH), const),
            pl.BlockSpec((1, LANES), const),
            pl.BlockSpec((1, LANES), const),
        ],
        out_specs=[
            pl.BlockSpec((TM, QK_DIM), tok),
            pl.BlockSpec((TM, QK_DIM), tok),
            pl.BlockSpec((TM, V_DIM), tok),
            pl.BlockSpec((TM, V_DIM), tok),
            pl.BlockSpec((TM, LANES), tok),
        ],
        out_shape=[
            jax.ShapeDtypeStruct((n_tok, QK_DIM), F32),
            jax.ShapeDtypeStruct((n_tok, QK_DIM), F32),
            jax.ShapeDtypeStruct((n_tok, V_DIM), F32),
            jax.ShapeDtypeStruct((n_tok, V_DIM), BF16),
            jax.ShapeDtypeStruct((n_tok, LANES), F32),
        ],
        compiler_params=_params(("arbitrary",)),
        name="dn_in",
    )(x, mod, n1, w, wba, cw, alog, dtb)


def _hdot(a, b):
    return jnp.dot(a, b, preferred_element_type=F32, precision=HIGHEST)


def _nt_dot(a, b):
    return lax.dot_general(a, b, (((1,), (1,)), ((), ())), preferred_element_type=F32)


def _delta_dir(bwd, q_ref, k_ref, v_ref, bg_ref, o_ref, state):
    r2 = lax.broadcasted_iota(jnp.int32, (CHUNK, LANES), 0)
    lane = lax.broadcasted_iota(jnp.int32, (CHUNK, LANES), 1)
    c2 = lane & (CHUNK - 1)
    left = lane < CHUNK
    causal2 = (r2 <= c2) if bwd else (r2 >= c2)
    strict2 = (r2 < c2) if bwd else (r2 > c2)
    eye2 = (r2 == c2).astype(F32)
    lvl = r2 ^ c2
    b0 = HV if bwd else 0
    g0 = 2 * HV + b0
    bg = bg_ref[...]
    gc = _hdot(causal2[:, :CHUNK].astype(F32), bg)
    gct = gc.T
    gtot = jnp.sum(bg, axis=0, keepdims=True)
    egc = jnp.exp(gc)
    eend = jnp.exp(gtot - gc)
    etot = jnp.exp(gtot)

    def pair_cols(a, j):
        return jnp.where(left, a[:, j:j + 1], a[:, j + 1:j + 2])

    def bdiag(x):
        z = jnp.zeros_like(x)
        return jnp.concatenate([jnp.where(left, x, z), jnp.where(left, z, x)], axis=0).astype(BF16)

    eye_k = (lax.broadcasted_iota(jnp.int32, (DK, DK), 0) == lax.broadcasted_iota(jnp.int32, (DK, DK), 1)).astype(BF16)
    ns, attns, kts = [], [], []
    for p in range(HK):
        j = g0 + PAIR * p
        k16 = k_ref[:, p * DK:(p + 1) * DK].astype(BF16)
        q16 = q_ref[:, p * DK:(p + 1) * DK].astype(BF16)
        kq = _nt_dot(jnp.concatenate([k16, q16], axis=0), jnp.concatenate([k16, k16], axis=0))
        grow2 = jnp.concatenate([gct[j:j + 1, :], gct[j + 1:j + 2, :]], axis=1)
        decay2 = jnp.exp(jnp.where(causal2, pair_cols(gc, j) - grow2, NEG))
        ns.append(jnp.where(strict2, kq[:CHUNK] * decay2 * pair_cols(bg, b0 + PAIR * p), 0.0))
        attns.append(kq[CHUNK:] * decay2)
        kts.append(_nt_dot(eye_k, k16).astype(BF16))
        yield

    def mdot(a, b):
        return jnp.dot(a.astype(BF16), bdiag(b), preferred_element_type=F32)

    n4 = [jnp.where(lvl < 4, n, 0.0) for n in ns]
    sq = [mdot(a, a) for a in n4]
    yield
    ts = [eye2 - a for a in n4]
    ts = [t + mdot(t, s) for t, s in zip(ts, sq)]
    yield
    for bit in range(2, 6):
        ys = [mdot(t, jnp.where((lvl >> bit) == 1, n, 0.0)) for t, n in zip(ts, ns)]
        yield
        ts = [t - mdot(y, t) for t, y in zip(ts, ys)]
        yield

    sols = []
    for p in range(HK):
        kh = k_ref[:, p * DK:(p + 1) * DK]
        rhs = []
        for e in range(PAIR):
            h = PAIR * p + e
            beta = bg[:, b0 + h:b0 + h + 1]
            rhs.append(jnp.concatenate(
                [v_ref[:, h * DV:(h + 1) * DV] * beta, kh * (beta * egc[:, g0 + h:g0 + h + 1])], axis=1))
        sols.append(jnp.dot(bdiag(ts[p]), jnp.concatenate(rhs, axis=0).astype(BF16), preferred_element_type=F32))
        yield

    outs = []
    for h in range(HV):
        p, e = divmod(h, PAIR)
        qg = q_ref[:, p * DK:(p + 1) * DK] * egc[:, g0 + h:g0 + h + 1]
        lhs = jnp.concatenate([sols[p][e * CHUNK:(e + 1) * CHUNK, DV:], qg], axis=0)
        outs.append(_bdot(lhs, state[bwd, h]))
        yield

    for p in range(HK):
        h0 = PAIR * p
        vns = [sols[p][e * CHUNK:(e + 1) * CHUNK, :DV] - outs[h0 + e][:CHUNK] for e in range(PAIR)]
        intra = jnp.dot(bdiag(attns[p]), jnp.concatenate(vns, axis=0).astype(BF16), preferred_element_type=F32)
        for e in range(PAIR):
            o_ref[:, (h0 + e) * DV:(h0 + e + 1) * DV] = (outs[h0 + e][CHUNK:] + intra[e * CHUNK:(e + 1) * CHUNK]).astype(BF16)
        scaled = jnp.concatenate([vns[e] * eend[:, g0 + h0 + e:g0 + h0 + e + 1] for e in range(PAIR)], axis=1)
        kv = jnp.dot(kts[p], scaled.astype(BF16), preferred_element_type=F32)
        for e in range(PAIR):
            h = h0 + e
            state[bwd, h] = state[bwd, h] * etot[:, g0 + h:g0 + h + 1] + kv[:, e * DV:(e + 1) * DV]
        yield


FIRST, LAST, DEC = 1, 2, 4


def _delta_body(fb_ref, bb_ref, si_ref, oi_ref, fl_ref, qf_ref, kf_ref, vf_ref, bgf_ref, qb_ref, kb_ref, vb_ref,
                bgb_ref, s0_ref, of_ref, ob_ref, sout_ref, state):
    flags = fl_ref[pl.program_id(0)]

    @pl.when((flags & FIRST) != 0)
    def _():
        state[...] = jnp.where((flags & DEC) != 0, s0_ref[...], 0.0)

    live = [_delta_dir(0, qf_ref, kf_ref, vf_ref, bgf_ref, of_ref, state),
            _delta_dir(1, qb_ref, kb_ref, vb_ref, bgb_ref, ob_ref, state)]
    while live:
        live = [g for g in live if next(g, StopIteration) is not StopIteration]

    @pl.when((flags & (LAST | DEC)) == LAST)
    def _():
        sout_ref[...] = state[...]


def _delta(q, k, v, bg, state_delta, n_ctx, ctx_chunks, n_dec, dec_chunks):
    n_tok = q.shape[0]
    fb, bb, si, oi, fl = [], [], [], [], []
    for dec, n_seq, n, base in ((0, n_ctx, ctx_chunks, 0), (1, n_dec, dec_chunks, n_ctx * ctx_chunks)):
        for b in range(n_seq):
            for c in range(n):
                fb.append(base + b * n + c)
                bb.append(base + b * n + n - 1 - c)
                si.append(b if dec else 0)
                oi.append(n_ctx - 1 if dec else b)
                fl.append((FIRST if c == 0 else 0) | (LAST if c == n - 1 else 0) | (DEC if dec else 0))
    tables = [jnp.asarray(t, jnp.int32) for t in (fb, bb, si, oi, fl)]
    fwd = lambda s, fb, bb, si, oi, fl: (fb[s], 0)
    bwd = lambda s, fb, bb, si, oi, fl: (bb[s], 0)
    tok_specs = lambda m: [pl.BlockSpec((CHUNK, QK_DIM), m), pl.BlockSpec((CHUNK, QK_DIM), m),
                           pl.BlockSpec((CHUNK, V_DIM), m), pl.BlockSpec((CHUNK, LANES), m)]
    return pl.pallas_call(
        _delta_body,
        grid_spec=pltpu.PrefetchScalarGridSpec(
            num_scalar_prefetch=5,
            grid=(len(fb),),
            in_specs=tok_specs(fwd) + tok_specs(bwd) + [
                pl.BlockSpec((None, None, 2, HV, DK, DV), lambda s, fb, bb, si, oi, fl: (si[s], 0, 0, 0, 0, 0)),
            ],
            out_specs=[
                pl.BlockSpec((CHUNK, V_DIM), fwd),
                pl.BlockSpec((CHUNK, V_DIM), bwd),
                pl.BlockSpec((None, 2, HV, DK, DV), lambda s, fb, bb, si, oi, fl: (oi[s], 0, 0, 0, 0)),
            ],
            scratch_shapes=[pltpu.VMEM((2, HV, DK, DV), F32)],
        ),
        out_shape=[
            jax.ShapeDtypeStruct((n_tok, V_DIM), BF16),
            jax.ShapeDtypeStruct((n_tok, V_DIM), BF16),
            jax.ShapeDtypeStruct((n_ctx, 2, HV, DK, DV), F32),
        ],
        compiler_params=_params(("arbitrary",)),
        name="delta_rule",
    )(*tables, q, k, v, bg, q, k, v, bg, state_delta)


def _dn_out_body(of_ref, ob_ref, z_ref, x_ref, mod_ref, nw_ref, wout_ref, n2_ref, x1_ref, h2_ref, h2p_ref):
    o = of_ref[...].astype(F32) + ob_ref[...].astype(F32)
    z = z_ref[...].astype(F32)
    parts = []
    for h in range(HV):
        oh = o[:, h * DV:(h + 1) * DV]
        nh = oh * lax.rsqrt(jnp.mean(oh * oh, axis=-1, keepdims=True) + EPS) * nw_ref[...]
        parts.append((nh * _silu(z[:, h * DV:(h + 1) * DV])).astype(BF16))
    mix = jnp.dot(jnp.concatenate(parts, axis=1), wout_ref[...], preferred_element_type=F32)
    _moe_input(x_ref[...] + mod_ref[2:3, :] * mix, mod_ref, n2_ref, x1_ref, h2_ref, h2p_ref)


def _dn_out(o_f, o_b, z, x, mod, nw, wout, n2, n_ctx_tiles, tiles_per_seq):
    n_tok = x.shape[0]
    tok, const, modm = _tile_maps(n_ctx_tiles, tiles_per_seq)
    out_specs, out_shapes = _mixer_outs(n_tok, tok)
    return pl.pallas_call(
        _dn_out_body,
        grid=(n_tok // TM,),
        in_specs=[
            pl.BlockSpec((TM, V_DIM), tok),
            pl.BlockSpec((TM, V_DIM), tok),
            pl.BlockSpec((TM, V_DIM), tok),
            pl.BlockSpec((TM, D), tok),
            pl.BlockSpec((None, 6, D), modm),
            pl.BlockSpec((1, DV), const),
            pl.BlockSpec((V_DIM, D), const),
            pl.BlockSpec((1, D), const),
        ],
        out_specs=out_specs,
        out_shape=out_shapes,
        compiler_params=_params(("arbitrary",)),
        name="dn_out",
    )(o_f, o_b, z, x, mod, nw, wout, n2)


def kernel(x_prompt, x_sample, state_delta, c, c_ctx, ada_w, ada_b, norm1_w, norm2_w, conv_in_w, conv_w,
           conv_out_w, dn_in_w, dn_conv_w, dn_a_log, dn_dt_bias, dn_norm_w, dn_out_w, router_w, router_b,
           exp_gu_w, exp_gu_b, exp_down_w, exp_down_b, final_norm_w):
    n_ctx, ctx_len, _ = x_prompt.shape
    n_dec, dec_len, _ = x_sample.shape
    assert ctx_len == CTX_ROW and dec_len % TM == 0 and n_dec + 1 <= N_COND
    n_ctx_tok = n_ctx * ctx_len
    n_ctx_tiles = n_ctx_tok // TM
    tiles_per_seq = dec_len // TM

    cond = jnp.zeros((N_COND, D), F32).at[0].set(c_ctx).at[1:1 + n_dec].set(c)
    mod = _ada(cond, ada_w, ada_b)

    def moe(l, x1, h2, h2p, final):
        return _moe(x1, h2, h2p, router_w[l].T, router_b[l][:, None], mod[l], final_norm_w[None, :], l, exp_gu_w,
                    exp_gu_b, exp_down_w, exp_down_b, n_ctx_tiles, tiles_per_seq, final)

    x1, h2, h2p = _conv_layer(x_prompt.reshape(n_ctx_tok, D), x_sample.reshape(n_dec * dec_len, D), mod[0],
                         norm1_w[0][None, :], conv_in_w[0].astype(BF16), conv_w[0], conv_out_w[0].astype(BF16),
                         norm2_w[0][None, :], n_ctx_tiles, tiles_per_seq)
    x, = moe(0, x1, h2, h2p, False)

    w_in = dn_in_w[0]
    w_ba = jnp.zeros((D, LANES), F32).at[:, :4 * HV].set(w_in[:, CONV_CH + V_DIM:]).astype(BF16)
    lane_pad = lambda a: jnp.zeros((1, LANES), F32).at[0, 2 * HV:4 * HV].set(a.reshape(-1))
    q, k, v, z, bg = _dn_in(x, mod[1], norm1_w[1][None, :], w_in[:, :CONV_CH + V_DIM].astype(BF16), w_ba,
                            dn_conv_w[0], lane_pad(dn_a_log[0]), lane_pad(dn_dt_bias[0]), n_ctx_tiles,
                            tiles_per_seq)
    o_f, o_b, s_ctx = _delta(q, k, v, bg, state_delta, n_ctx, ctx_len // CHUNK, n_dec, dec_len // CHUNK)
    x1, h2, h2p = _dn_out(o_f, o_b, z, x, mod[1], dn_norm_w[0][None, :], dn_out_w[0].astype(BF16),
                     norm2_w[1][None, :], n_ctx_tiles, tiles_per_seq)
    y_ctx, y_dec = moe(1, x1, h2, h2p, True)
    return y_ctx.reshape(n_ctx, ctx_len, D), y_dec.reshape(n_dec, dec_len, D), s_ctx[:, None]
```

```python
import functools

import jax
import jax.numpy as jnp
from jax import lax
from jax.experimental import pallas as pl
from jax.experimental.pallas import tpu as pltpu

F32 = jnp.float32
BF16 = jnp.bfloat16

D = 1024
TM = 256
CTX_ROW = 256
GRID_W = 64
HK, HV, DK, DV = 8, 16, 128, 128
QK_DIM = HK * DK
V_DIM = HV * DV
CONV_CH = 2 * QK_DIM + V_DIM
CHUNK = 64
N_EXPERTS = 32
TOP_K = 4
D_FF = 1024
DP = D // 2
SWIGLU_ALPHA = 1.702
SWIGLU_LIMIT = 7.0
EPS = 1e-6
LANES = 128
N_COND = 8
BLK = 512
NEG = -3.0e38
VMEM_LIMIT = 56 * 1024 * 1024
HIGHEST = lax.Precision.HIGHEST
G0 = 2 * HV
PAIR = HV // HK


def _silu(x):
    return x * jax.nn.sigmoid(x)


def _bdot(a, b):
    return jnp.dot(a.astype(BF16), b.astype(BF16), preferred_element_type=F32)


def _pack_bf16(x):
    m = x.shape[1] // 2

    def rne(v):
        bits = lax.bitcast_convert_type(v, jnp.uint32)
        return (bits + jnp.uint32(0x7FFF) + ((bits >> 16) & jnp.uint32(1))) >> 16

    return rne(x[:, :m]) | (rne(x[:, m:]) << 16)


def _unpack_bf16(p):
    lo = lax.bitcast_convert_type(p << 16, F32)
    hi = lax.bitcast_convert_type(p & jnp.uint32(0xFFFF0000), F32)
    return jnp.concatenate([lo.astype(BF16), hi.astype(BF16)], axis=1)


def _params(sem):
    return pltpu.CompilerParams(dimension_semantics=sem, vmem_limit_bytes=VMEM_LIMIT)


def _ada_body(c_ref, w_ref, b_ref, o_ref):
    o_ref[...] = _bdot(_silu(c_ref[...]), w_ref[...]) + b_ref[...]


def _ada(cond, ada_w, ada_b):
    n_layers = ada_w.shape[0]
    tn = 1024
    out = pl.pallas_call(
        _ada_body,
        grid=(n_layers, 6 * D // tn),
        in_specs=[
            pl.BlockSpec((N_COND, D), lambda l, j: (0, 0)),
            pl.BlockSpec((None, D, tn), lambda l, j: (l, 0, j)),
            pl.BlockSpec((None, 1, tn), lambda l, j: (l, 0, j)),
        ],
        out_specs=pl.BlockSpec((None, N_COND, tn), lambda l, j: (l, 0, j)),
        out_shape=jax.ShapeDtypeStruct((n_layers, N_COND, 6 * D), F32),
        compiler_params=_params(("arbitrary", "arbitrary")),
        name="ada",
    )(cond, ada_w, ada_b.reshape(n_layers, 1, 6 * D))
    return out.reshape(n_layers, N_COND, 6, D)


def _prenorm(x, nw, sc, sh):
    ms = jnp.mean(x * x, axis=-1, keepdims=True)
    return (x * lax.rsqrt(ms + EPS) * nw) * (1.0 + sc) + sh


def _short_conv(v, w_ref, row_len):
    n = v.shape[0]
    pos = lax.broadcasted_iota(jnp.int32, (n, 1), 0) & (row_len - 1)
    prev = jnp.where(pos == 0, 0.0, pltpu.roll(v, 1, 0))
    nxt = jnp.where(pos == row_len - 1, 0.0, pltpu.roll(v, n - 1, 0))
    return prev * w_ref[0:1, :] + v * w_ref[1:2, :] + nxt * w_ref[2:3, :]


RT = 1024


def _route_body(h_ref, rwt_ref, rb_ref, idx_ref, lrank_ref, gate_ref, cnt_ref):
    logits = lax.dot_general(rwt_ref[...], h_ref[...], (((1,), (1,)), ((), ())), preferred_element_type=F32,
                             precision=HIGHEST) + rb_ref[...]
    eid = lax.broadcasted_iota(jnp.int32, (N_EXPERTS, RT), 0)
    work = logits
    sel = jnp.zeros((N_EXPERTS, RT), F32)
    vals, ids, hots = [], [], []
    for _ in range(TOP_K):
        m = jnp.max(work, axis=0, keepdims=True)
        ik = jnp.min(jnp.where(work == m, eid, N_EXPERTS), axis=0, keepdims=True)
        hot = eid == ik
        work = jnp.where(hot, NEG, work)
        sel = sel + hot.astype(F32)
        vals.append(m)
        ids.append(ik)
        hots.append(hot)
    es = [jnp.exp(v - vals[0]) for v in vals]
    den = es[0] + es[1] + es[2] + es[3]
    earlier = (lax.broadcasted_iota(jnp.int32, (TM, TM), 0) < lax.broadcasted_iota(jnp.int32, (TM, TM), 1))
    earlier = earlier.astype(BF16)
    ranks, cnts = [], []
    for j in range(RT // TM):
        part = sel[:, j * TM:(j + 1) * TM]
        ranks.append(jnp.dot(part.astype(BF16), earlier, preferred_element_type=F32))
        cnts.append(jnp.broadcast_to(jnp.sum(part, axis=1, keepdims=True), (N_EXPERTS, TM)))
    rank_all = jnp.concatenate(ranks, axis=1)
    cnt_ref[...] = jnp.concatenate(cnts, axis=1).astype(jnp.int32)
    slot = lax.broadcasted_iota(jnp.int32, (8, RT), 0)
    idx_o = jnp.zeros((8, RT), jnp.int32)
    rank_o = jnp.zeros((8, RT), jnp.int32)
    gate_o = jnp.zeros((8, RT), F32)
    for k in range(TOP_K):
        rk = jnp.sum(jnp.where(hots[k], rank_all, 0.0), axis=0, keepdims=True)
        idx_o = jnp.where(slot == k, ids[k], idx_o)
        rank_o = jnp.where(slot == k, rk.astype(jnp.int32), rank_o)
        gate_o = jnp.where(slot == k, es[k] / den, gate_o)
    idx_ref[...] = idx_o
    lrank_ref[...] = rank_o
    gate_ref[...] = gate_o


def _route(h2, rwt, rb):
    n_tok = h2.shape[0]
    const = lambda i: (0, 0)
    cols = lambda i: (0, i)
    return pl.pallas_call(
        _route_body,
        grid=(n_tok // RT,),
        in_specs=[
            pl.BlockSpec((RT, D), lambda i: (i, 0)),
            pl.BlockSpec((N_EXPERTS, D), const),
            pl.BlockSpec((N_EXPERTS, 1), const),
        ],
        out_specs=[
            pl.BlockSpec((8, RT), cols),
            pl.BlockSpec((8, RT), cols),
            pl.BlockSpec((8, RT), cols),
            pl.BlockSpec((N_EXPERTS, RT), cols),
        ],
        out_shape=[
            jax.ShapeDtypeStruct((8, n_tok), jnp.int32),
            jax.ShapeDtypeStruct((8, n_tok), jnp.int32),
            jax.ShapeDtypeStruct((8, n_tok), F32),
            jax.ShapeDtypeStruct((N_EXPERTS, n_tok), jnp.int32),
        ],
        compiler_params=_params(("arbitrary",)),
        name="moe_route",
    )(h2, rwt, rb)


def _tile_maps(n_ctx_tiles, tiles_per_seq):
    def grp(i):
        return jnp.where(i < n_ctx_tiles, 0, 1 + (i - n_ctx_tiles) // tiles_per_seq)

    tok = lambda i: (i, 0)
    const = lambda i: (0, 0)
    mod = lambda i: (grp(i), 0, 0)
    return tok, const, mod


def _mixer_outs(n_tok, tok):
    return ([pl.BlockSpec((TM, D), tok), pl.BlockSpec((TM, D), tok)],
            [jax.ShapeDtypeStruct((n_tok, D), F32), jax.ShapeDtypeStruct((n_tok, D), F32)])


def _moe_input(x1, mod_ref, n2_ref, x1_ref, h2_ref):
    x1_ref[...] = x1
    h2_ref[...] = _prenorm(x1, n2_ref[...], mod_ref[4:5, :], mod_ref[3:4, :])


def _conv_body(n_ctx_tiles, xp_ref, xs_ref, mod_ref, n1_ref, win_ref, cw_ref, wout_ref, n2_ref, x1_ref, h2_ref):
    i = pl.program_id(0)
    row_len = jnp.where(i < n_ctx_tiles, CTX_ROW, GRID_W)
    x = jnp.where(i < n_ctx_tiles, xp_ref[...], xs_ref[...])
    h = _prenorm(x, n1_ref[...], mod_ref[1:2, :], mod_ref[0:1, :])
    proj = _bdot(h, win_ref[...])
    b, cg, u = proj[:, :D], proj[:, D:2 * D], proj[:, 2 * D:]
    mix = _bdot(b * _short_conv(cg * u, cw_ref, row_len), wout_ref[...])
    _moe_input(x + mod_ref[2:3, :] * mix, mod_ref, n2_ref, x1_ref, h2_ref)


def _conv_layer(x_ctx, x_dec, mod, n1, win, cw, wout, n2, n_ctx_tiles, tiles_per_seq):
    n_tok = x_ctx.shape[0] + x_dec.shape[0]
    tok, const, modm = _tile_maps(n_ctx_tiles, tiles_per_seq)
    out_specs, out_shapes = _mixer_outs(n_tok, tok)
    return pl.pallas_call(
        functools.partial(_conv_body, n_ctx_tiles),
        grid=(n_tok // TM,),
        in_specs=[
            pl.BlockSpec((TM, D), lambda i: (jnp.minimum(i, n_ctx_tiles - 1), 0)),
            pl.BlockSpec((TM, D), lambda i: (jnp.maximum(i - n_ctx_tiles, 0), 0)),
            pl.BlockSpec((None, 6, D), modm),
            pl.BlockSpec((1, D), const),
            pl.BlockSpec((D, 3 * D), const),
            pl.BlockSpec((3, D), const),
            pl.BlockSpec((D, D), const),
            pl.BlockSpec((1, D), const),
        ],
        out_specs=out_specs,
        out_shape=out_shapes,
        compiler_params=_params(("arbitrary",)),
        name="conv_layer",
    )(x_ctx, x_dec, mod, n1, win, cw, wout, n2)


RUN_ALIGN = 8
RUN_SIZES = (256, 128, 64, 32, 16, 8)
SORT_ROWS = TM * TOP_K + N_EXPERTS * RUN_ALIGN


def _run_pieces(src_ref, dst_ref, len_ref, tile):
    for e in range(N_EXPERTS):
        n = len_ref[tile * N_EXPERTS + e]
        src = src_ref[tile * N_EXPERTS + e]
        dst = dst_ref[tile * N_EXPERTS + e]
        for size in RUN_SIZES:
            done = n & ~(2 * size - 1)
            yield ((n & size) != 0, pl.multiple_of(src + done, RUN_ALIGN), pl.multiple_of(dst + done, RUN_ALIGN),
                   size)


def _sorted_onehot(lpos, rows_major):
    if rows_major:
        j = lax.broadcasted_iota(jnp.int32, (SORT_ROWS, TM), 0)
        return [j == lpos[k:k + 1, :] for k in range(TOP_K)]
    j = lax.broadcasted_iota(jnp.int32, (TM, SORT_ROWS), 1)
    return [j == lpos[:, k:k + 1] for k in range(TOP_K)]


def _scatter_body(src_ref, dst_ref, len_ref, pend_ref, lpos_ref, h_ref, xs_ref, sorted_ref, zbuf, sem, zsem):
    i = pl.program_id(0)

    @pl.when(i == 0)
    def _():
        zbuf[...] = jnp.zeros_like(zbuf)

        def zero_block(j):
            return pltpu.make_async_copy(zbuf, xs_ref.at[pl.ds(pl.multiple_of(j * BLK, BLK), BLK)], zsem)

        def start(j, carry):
            zero_block(j).start()
            return carry

        def wait(j, carry):
            zero_block(j).wait()
            return carry

        n_used = pend_ref[N_EXPERTS - 1] // BLK
        lax.fori_loop(n_used, xs_ref.shape[0] // BLK, start, 0)
        lax.fori_loop(n_used, xs_ref.shape[0] // BLK, wait, 0)
        for phase in range(2):
            for e in range(N_EXPERTS):
                end = pend_ref[e]
                begin = pend_ref[e - 1] if e else 0

                @pl.when(end > begin)
                def _():
                    tail = pl.ds(pl.multiple_of(end - BLK, BLK), BLK)
                    cp = pltpu.make_async_copy(zbuf, xs_ref.at[tail], zsem)
                    if phase == 0:
                        cp.start()
                    else:
                        cp.wait()

    hots = _sorted_onehot(lpos_ref[...], True)
    onehot = (hots[0] | hots[1] | hots[2] | hots[3]).astype(BF16)
    sorted_ref[...] = _pack_bf16(jnp.dot(onehot, h_ref[...].astype(BF16), preferred_element_type=F32))
    for phase in range(2):
        for cond, src, dst, size in _run_pieces(src_ref, dst_ref, len_ref, i):
            @pl.when(cond)
            def _():
                cp = pltpu.make_async_copy(sorted_ref.at[pl.ds(src, size)], xs_ref.at[pl.ds(dst, size)], sem)
                if phase == 0:
                    cp.start()
                else:
                    cp.wait()


def _scatter(src_start, dst_start, run_len, pend, lpos, h2, n_slots):
    n_tok = h2.shape[0]
    return pl.pallas_call(
        _scatter_body,
        grid_spec=pltpu.PrefetchScalarGridSpec(
            num_scalar_prefetch=4,
            grid=(n_tok // TM,),
            in_specs=[pl.BlockSpec((8, TM), lambda i, *_: (0, i)), pl.BlockSpec((TM, D), lambda i, *_: (i, 0))],
            out_specs=pl.BlockSpec(memory_space=pl.ANY),
            scratch_shapes=[pltpu.VMEM((SORT_ROWS, DP), jnp.uint32), pltpu.VMEM((BLK, DP), jnp.uint32),
                            pltpu.SemaphoreType.DMA, pltpu.SemaphoreType.DMA],
        ),
        out_shape=jax.ShapeDtypeStruct((n_slots, DP), jnp.uint32),
        compiler_params=_params(("arbitrary",)),
        name="moe_scatter",
    )(src_start, dst_start, run_len, pend, lpos, h2)


W_ROWS = 128


def _expert_body(be_ref, nu_ref, x_ref, wgu_ref, bgu_ref, wd_ref, bd_ref, y_ref, wgu_bf, wd_bf):
    j = pl.program_id(0)
    n_used = nu_ref[0]
    jj = jnp.minimum(j, n_used - 1)
    e = be_ref[jj]
    e_prev = be_ref[jnp.maximum(jj - 1, 0)]

    @pl.when((j == 0) | (e != e_prev))
    def _():
        def cast(r, carry):
            rows = pl.ds(pl.multiple_of(r * W_ROWS, W_ROWS), W_ROWS)
            wgu_bf[rows, :] = wgu_ref[rows, :].astype(BF16)
            wd_bf[rows, :] = wd_ref[rows, :].astype(BF16)
            return carry

        lax.fori_loop(0, D // W_ROWS, cast, 0)

    @pl.when(j < n_used)
    def _():
        hh = jnp.dot(_unpack_bf16(x_ref[...]), wgu_bf[...], preferred_element_type=F32) + bgu_ref[...]
        hg = jnp.minimum(hh[:, :D_FF], SWIGLU_LIMIT)
        hl = jnp.clip(hh[:, D_FF:], -SWIGLU_LIMIT, SWIGLU_LIMIT)
        act = hg * jax.nn.sigmoid(SWIGLU_ALPHA * hg) * (hl + 1.0)
        y_ref[...] = _pack_bf16(jnp.dot(act.astype(BF16), wd_bf[...], preferred_element_type=F32) + bd_ref[...])

    @pl.when(j >= n_used)
    def _():
        y_ref[...] = jnp.zeros_like(y_ref)


def _experts(block_e, n_used, xs, layer, wgu, bgu, wd, bd):
    n_slots = xs.shape[0]
    n_layers = wgu.shape[0]

    def blk(j, be, nu):
        return (jnp.minimum(j, nu[0] - 1), 0)

    def wmap(j, be, nu):
        return (layer, be[jnp.minimum(j, nu[0] - 1)], 0, 0)

    return pl.pallas_call(
        _expert_body,
        grid_spec=pltpu.PrefetchScalarGridSpec(
            num_scalar_prefetch=2,
            grid=(n_slots // BLK,),
            in_specs=[
                pl.BlockSpec((BLK, DP), blk),
                pl.BlockSpec((None, None, D, 2 * D_FF), wmap),
                pl.BlockSpec((None, None, 1, 2 * D_FF), wmap),
                pl.BlockSpec((None, None, D_FF, D), wmap),
                pl.BlockSpec((None, None, 1, D), wmap),
            ],
            out_specs=pl.BlockSpec((BLK, DP), lambda j, be, nu: (j, 0)),
            scratch_shapes=[pltpu.VMEM((D, 2 * D_FF), BF16), pltpu.VMEM((D_FF, D), BF16)],
        ),
        out_shape=jax.ShapeDtypeStruct((n_slots, DP), jnp.uint32),
        compiler_params=_params(("arbitrary",)),
        name="moe_experts",
    )(block_e, n_used, xs, wgu, bgu.reshape(n_layers, N_EXPERTS, 1, 2 * D_FF), wd,
      bd.reshape(n_layers, N_EXPERTS, 1, D))


def _combine_body(n_ctx_tiles, src_ref, dst_ref, len_ref, x_ref, lpos_ref, gate_ref, mod_ref, fnw_ref, y_ref, *rest):
    *o_refs, buf, sem = rest
    i = pl.program_id(0)
    n_tiles = pl.num_programs(0)

    def runs(tile, slot, start):
        for cond, src, dst, size in _run_pieces(src_ref, dst_ref, len_ref, tile):
            @pl.when(cond)
            def _():
                cp = pltpu.make_async_copy(y_ref.at[pl.ds(dst, size)], buf.at[slot, pl.ds(src, size)], sem.at[slot])
                if start:
                    cp.start()
                else:
                    cp.wait()

    @pl.when(i == 0)
    def _():
        buf[...] = jnp.zeros_like(buf)
        runs(0, 0, True)

    @pl.when(i + 1 < n_tiles)
    def _():
        runs(i + 1, (i + 1) % 2, True)

    slot = i % 2
    runs(i, slot, False)
    gate = gate_ref[...].T
    hots = _sorted_onehot(lpos_ref[...].T, False)
    placed = jnp.where(hots[0], gate[:, 0:1], 0.0)
    for k in range(1, TOP_K):
        placed = jnp.where(hots[k], gate[:, k:k + 1], placed)
    acc = jnp.dot(placed.astype(BF16), _unpack_bf16(buf[slot]), preferred_element_type=F32)
    x2 = x_ref[...] + mod_ref[5:6, :] * acc
    if n_ctx_tiles is None:
        o_refs[0][...] = x2
    else:
        ms = jnp.mean(x2 * x2, axis=-1, keepdims=True)
        y = x2 * lax.rsqrt(ms + EPS) * fnw_ref[...]

        @pl.when(i < n_ctx_tiles)
        def _():
            o_refs[0][...] = y

        @pl.when(i >= n_ctx_tiles)
        def _():
            o_refs[1][...] = y


def _combine(src_start, dst_start, run_len, x1, lpos, gate, mod, fnw, y_slots, n_ctx_tiles, tiles_per_seq, final):
    n_tok = x1.shape[0]

    def grp(i):
        return jnp.where(i < n_ctx_tiles, 0, 1 + (i - n_ctx_tiles) // tiles_per_seq)

    if final:
        out_specs = [pl.BlockSpec((TM, D), lambda i, *_: (jnp.minimum(i, n_ctx_tiles - 1), 0)),
                     pl.BlockSpec((TM, D), lambda i, *_: (jnp.maximum(i - n_ctx_tiles, 0), 0))]
        out_shape = [jax.ShapeDtypeStruct((n_ctx_tiles * TM, D), F32),
                     jax.ShapeDtypeStruct((n_tok - n_ctx_tiles * TM, D), F32)]
    else:
        out_specs = [pl.BlockSpec((TM, D), lambda i, *_: (i, 0))]
        out_shape = [jax.ShapeDtypeStruct((n_tok, D), F32)]
    return pl.pallas_call(
        functools.partial(_combine_body, n_ctx_tiles if final else None),
        grid_spec=pltpu.PrefetchScalarGridSpec(
            num_scalar_prefetch=3,
            grid=(n_tok // TM,),
            in_specs=[
                pl.BlockSpec((TM, D), lambda i, *_: (i, 0)),
                pl.BlockSpec((8, TM), lambda i, *_: (0, i)),
                pl.BlockSpec((8, TM), lambda i, *_: (0, i)),
                pl.BlockSpec((None, 6, D), lambda i, *_: (grp(i), 0, 0)),
                pl.BlockSpec((1, D), lambda i, *_: (0, 0)),
                pl.BlockSpec(memory_space=pl.ANY),
            ],
            out_specs=out_specs,
            scratch_shapes=[pltpu.VMEM((2, SORT_ROWS, DP), jnp.uint32), pltpu.SemaphoreType.DMA((2,))],
        ),
        out_shape=out_shape,
        compiler_params=_params(("arbitrary",)),
        name="moe_combine",
    )(src_start, dst_start, run_len, x1, lpos, gate, mod, fnw, y_slots)


def _moe(x1, h2, rwt, rb, mod, fnw, layer, wgu, bgu, wd, bd, n_ctx_tiles, tiles_per_seq, final):
    n_tok = x1.shape[0]
    n_tiles = n_tok // TM
    idx, lrank, gate, cnt_tok = _route(h2, rwt, rb)
    cnt = cnt_tok[:, ::TM].T
    run_len = (cnt + RUN_ALIGN - 1) // RUN_ALIGN * RUN_ALIGN
    src_start = jnp.cumsum(run_len, axis=1) - run_len
    total = jnp.sum(run_len, axis=0)
    padded = (total + BLK - 1) // BLK * BLK
    pend = jnp.cumsum(padded)
    dst_start = (pend - padded)[None, :] + jnp.cumsum(run_len, axis=0) - run_len
    hit = idx[:TOP_K, None, :] == jnp.arange(N_EXPERTS, dtype=jnp.int32)[None, :, None]
    src_tok = jnp.repeat(src_start.T, TM, axis=1)
    lpos = jnp.sum(jnp.where(hit, src_tok[None], 0), axis=1) + lrank[:TOP_K]
    lpos = jnp.concatenate([lpos, jnp.full((8 - TOP_K, n_tok), -1, jnp.int32)], axis=0).astype(jnp.int32)
    n_slots = (n_tok * TOP_K + n_tiles * N_EXPERTS * (RUN_ALIGN - 1)) // BLK * BLK + (N_EXPERTS + 1) * BLK
    n_blocks = n_slots // BLK
    block_start = jnp.arange(n_blocks, dtype=jnp.int32)[:, None] * BLK
    block_e = jnp.minimum(jnp.sum(pend[None, :] <= block_start, axis=1), N_EXPERTS - 1).astype(jnp.int32)
    n_used = (pend[-1:] // BLK).astype(jnp.int32)
    tables = [t.reshape(-1).astype(jnp.int32) for t in (src_start, dst_start, run_len)]
    xs = _scatter(*tables, pend.astype(jnp.int32), lpos, h2, n_slots)
    ys = _experts(block_e, n_used, xs, layer, wgu, bgu, wd, bd)
    return _combine(*tables, x1, lpos, gate, mod, fnw, ys, n_ctx_tiles, tiles_per_seq, final)


def _dn_in_body(n_ctx_tiles, x_ref, mod_ref, n1_ref, w_ref, wba_ref, cw_ref, alog_ref, dtb_ref,
                q_ref, k_ref, v_ref, z_ref, bg_ref):
    i = pl.program_id(0)
    row_len = jnp.where(i < n_ctx_tiles, CTX_ROW, GRID_W)
    h = _prenorm(x_ref[...], n1_ref[...], mod_ref[1:2, :], mod_ref[0:1, :]).astype(BF16)
    proj = jnp.dot(h, w_ref[...], preferred_element_type=F32)
    z_ref[...] = proj[:, CONV_CH:].astype(BF16)
    qkv = _silu(_short_conv(proj[:, :CONV_CH], cw_ref, row_len))
    for hd in range(2 * HK):
        s = qkv[:, hd * DK:(hd + 1) * DK]
        n = s * lax.rsqrt(jnp.sum(s * s, axis=-1, keepdims=True) + EPS)
        if hd < HK:
            q_ref[:, hd * DK:(hd + 1) * DK] = n * (DK ** -0.5)
        else:
            k_ref[:, (hd - HK) * DK:(hd - HK + 1) * DK] = n
    v_ref[...] = qkv[:, 2 * QK_DIM:]
    ba = jnp.dot(h, wba_ref[...], preferred_element_type=F32)
    beta = jax.nn.sigmoid(ba)
    a = ba + dtb_ref[...]
    softplus = jnp.maximum(a, 0.0) + jnp.log(1.0 + jnp.exp(-jnp.abs(a)))
    g = -jnp.exp(alog_ref[...]) * softplus
    lane = lax.broadcasted_iota(jnp.int32, (TM, LANES), 1)
    bg_ref[...] = jnp.where(lane < G0, beta, g)


def _dn_in(x, mod, n1, w, wba, cw, alog, dtb, n_ctx_tiles, tiles_per_seq):
    n_tok = x.shape[0]
    tok, const, modm = _tile_maps(n_ctx_tiles, tiles_per_seq)
    return pl.pallas_call(
        functools.partial(_dn_in_body, n_ctx_tiles),
        grid=(n_tok // TM,),
        in_specs=[
            pl.BlockSpec((TM, D), tok),
            pl.BlockSpec((None, 6, D), modm),
            pl.BlockSpec((1, D), const),
            pl.BlockSpec((D, CONV_CH + V_DIM), const),
            pl.BlockSpec((D, LANES), const),
            pl.BlockSpec((3, CONV_CH), const),
            pl.BlockSpec((1, LANES), const),
            pl.BlockSpec((1, LANES), const),
        ],
        out_specs=[
            pl.BlockSpec((TM, QK_DIM), tok),
            pl.BlockSpec((TM, QK_DIM), tok),
            pl.BlockSpec((TM, V_DIM), tok),
            pl.BlockSpec((TM, V_DIM), tok),
            pl.BlockSpec((TM, LANES), tok),
        ],
        out_shape=[
            jax.ShapeDtypeStruct((n_tok, QK_DIM), F32),
            jax.ShapeDtypeStruct((n_tok, QK_DIM), F32),
            jax.ShapeDtypeStruct((n_tok, V_DIM), F32),
            jax.ShapeDtypeStruct((n_tok, V_DIM), BF16),
            jax.ShapeDtypeStruct((n_tok, LANES), F32),
        ],
        compiler_params=_params(("arbitrary",)),
        name="dn_in",
    )(x, mod, n1, w, wba, cw, alog, dtb)


def _hdot(a, b):
    return jnp.dot(a, b, preferred_element_type=F32, precision=HIGHEST)


def _nt_dot(a, b):
    return lax.dot_general(a, b, (((1,), (1,)), ((), ())), preferred_element_type=F32)


def _delta_dir(bwd, q_ref, k_ref, v_ref, bg_ref, o_ref, state):
    r2 = lax.broadcasted_iota(jnp.int32, (CHUNK, LANES), 0)
    lane = lax.broadcasted_iota(jnp.int32, (CHUNK, LANES), 1)
    c2 = lane & (CHUNK - 1)
    left = lane < CHUNK
    causal2 = (r2 <= c2) if bwd else (r2 >= c2)
    strict2 = (r2 < c2) if bwd else (r2 > c2)
    eye2 = (r2 == c2).astype(F32)
    lvl = r2 ^ c2
    b0 = HV if bwd else 0
    g0 = 2 * HV + b0
    bg = bg_ref[...]
    gc = _hdot(causal2[:, :CHUNK].astype(F32), bg)
    gct = gc.T
    gtot = jnp.sum(bg, axis=0, keepdims=True)
    egc = jnp.exp(gc)
    eend = jnp.exp(gtot - gc)
    etot = jnp.exp(gtot)

    def pair_cols(a, j):
        return jnp.where(left, a[:, j:j + 1], a[:, j + 1:j + 2])

    def bdiag(x):
        z = jnp.zeros_like(x)
        return jnp.concatenate([jnp.where(left, x, z), jnp.where(left, z, x)], axis=0).astype(BF16)

    eye_k = (lax.broadcasted_iota(jnp.int32, (DK, DK), 0) == lax.broadcasted_iota(jnp.int32, (DK, DK), 1)).astype(BF16)
    ns, attns, kts = [], [], []
    for p in range(HK):
        j = g0 + PAIR * p
        k16 = k_ref[:, p * DK:(p + 1) * DK].astype(BF16)
        q16 = q_ref[:, p * DK:(p + 1) * DK].astype(BF16)
        kq = _nt_dot(jnp.concatenate([k16, q16], axis=0), jnp.concatenate([k16, k16], axis=0))
        grow2 = jnp.concatenate([gct[j:j + 1, :], gct[j + 1:j + 2, :]], axis=1)
        decay2 = jnp.exp(jnp.where(causal2, pair_cols(gc, j) - grow2, NEG))
        ns.append(jnp.where(strict2, kq[:CHUNK] * decay2 * pair_cols(bg, b0 + PAIR * p), 0.0))
        attns.append(kq[CHUNK:] * decay2)
        kts.append(_nt_dot(eye_k, k16).astype(BF16))
        yield

    def mdot(a, b):
        return jnp.dot(a.astype(BF16), bdiag(b), preferred_element_type=F32)

    n4 = [jnp.where(lvl < 4, n, 0.0) for n in ns]
    sq = [mdot(a, a) for a in n4]
    yield
    ts = [eye2 - a for a in n4]
    ts = [t + mdot(t, s) for t, s in zip(ts, sq)]
    yield
    for bit in range(2, 6):
        ys = [mdot(t, jnp.where((lvl >> bit) == 1, n, 0.0)) for t, n in zip(ts, ns)]
        yield
        ts = [t - mdot(y, t) for t, y in zip(ts, ys)]
        yield

    sols = []
    for p in range(HK):
        kh = k_ref[:, p * DK:(p + 1) * DK]
        rhs = []
        for e in range(PAIR):
            h = PAIR * p + e
            beta = bg[:, b0 + h:b0 + h + 1]
            rhs.append(jnp.concatenate(
                [v_ref[:, h * DV:(h + 1) * DV] * beta, kh * (beta * egc[:, g0 + h:g0 + h + 1])], axis=1))
        sols.append(jnp.dot(bdiag(ts[p]), jnp.concatenate(rhs, axis=0).astype(BF16), preferred_element_type=F32))
        yield

    outs = []
    for h in range(HV):
        p, e = divmod(h, PAIR)
        qg = q_ref[:, p * DK:(p + 1) * DK] * egc[:, g0 + h:g0 + h + 1]
        lhs = jnp.concatenate([sols[p][e * CHUNK:(e + 1) * CHUNK, DV:], qg], axis=0)
        outs.append(_bdot(lhs, state[bwd, h]))
        yield

    for p in range(HK):
        h0 = PAIR * p
        vns = [sols[p][e * CHUNK:(e + 1) * CHUNK, :DV] - outs[h0 + e][:CHUNK] for e in range(PAIR)]
        intra = jnp.dot(bdiag(attns[p]), jnp.concatenate(vns, axis=0).astype(BF16), preferred_element_type=F32)
        for e in range(PAIR):
            o_ref[:, (h0 + e) * DV:(h0 + e + 1) * DV] = (outs[h0 + e][CHUNK:] + intra[e * CHUNK:(e + 1) * CHUNK]).astype(BF16)
        scaled = jnp.concatenate([vns[e] * eend[:, g0 + h0 + e:g0 + h0 + e + 1] for e in range(PAIR)], axis=1)
        kv = jnp.dot(kts[p], scaled.astype(BF16), preferred_element_type=F32)
        for e in range(PAIR):
            h = h0 + e
            state[bwd, h] = state[bwd, h] * etot[:, g0 + h:g0 + h + 1] + kv[:, e * DV:(e + 1) * DV]
        yield


FIRST, LAST, DEC = 1, 2, 4


def _delta_body(fb_ref, bb_ref, si_ref, oi_ref, fl_ref, qf_ref, kf_ref, vf_ref, bgf_ref, qb_ref, kb_ref, vb_ref,
                bgb_ref, s0_ref, of_ref, ob_ref, sout_ref, state):
    flags = fl_ref[pl.program_id(0)]

    @pl.when((flags & FIRST) != 0)
    def _():
        state[...] = jnp.where((flags & DEC) != 0, s0_ref[...], 0.0)

    live = [_delta_dir(0, qf_ref, kf_ref, vf_ref, bgf_ref, of_ref, state),
            _delta_dir(1, qb_ref, kb_ref, vb_ref, bgb_ref, ob_ref, state)]
    while live:
        live = [g for g in live if next(g, StopIteration) is not StopIteration]

    @pl.when((flags & (LAST | DEC)) == LAST)
    def _():
        sout_ref[...] = state[...]


def _delta(q, k, v, bg, state_delta, n_ctx, ctx_chunks, n_dec, dec_chunks):
    n_tok = q.shape[0]
    fb, bb, si, oi, fl = [], [], [], [], []
    for dec, n_seq, n, base in ((0, n_ctx, ctx_chunks, 0), (1, n_dec, dec_chunks, n_ctx * ctx_chunks)):
        for b in range(n_seq):
            for c in range(n):
                fb.append(base + b * n + c)
                bb.append(base + b * n + n - 1 - c)
                si.append(b if dec else 0)
                oi.append(n_ctx - 1 if dec else b)
                fl.append((FIRST if c == 0 else 0) | (LAST if c == n - 1 else 0) | (DEC if dec else 0))
    tables = [jnp.asarray(t, jnp.int32) for t in (fb, bb, si, oi, fl)]
    fwd = lambda s, fb, bb, si, oi, fl: (fb[s], 0)
    bwd = lambda s, fb, bb, si, oi, fl: (bb[s], 0)
    tok_specs = lambda m: [pl.BlockSpec((CHUNK, QK_DIM), m), pl.BlockSpec((CHUNK, QK_DIM), m),
                           pl.BlockSpec((CHUNK, V_DIM), m), pl.BlockSpec((CHUNK, LANES), m)]
    return pl.pallas_call(
        _delta_body,
        grid_spec=pltpu.PrefetchScalarGridSpec(
            num_scalar_prefetch=5,
            grid=(len(fb),),
            in_specs=tok_specs(fwd) + tok_specs(bwd) + [
                pl.BlockSpec((None, None, 2, HV, DK, DV), lambda s, fb, bb, si, oi, fl: (si[s], 0, 0, 0, 0, 0)),
            ],
            out_specs=[
                pl.BlockSpec((CHUNK, V_DIM), fwd),
                pl.BlockSpec((CHUNK, V_DIM), bwd),
                pl.BlockSpec((None, 2, HV, DK, DV), lambda s, fb, bb, si, oi, fl: (oi[s], 0, 0, 0, 0)),
            ],
            scratch_shapes=[pltpu.VMEM((2, HV, DK, DV), F32)],
        ),
        out_shape=[
            jax.ShapeDtypeStruct((n_tok, V_DIM), BF16),
            jax.ShapeDtypeStruct((n_tok, V_DIM), BF16),
            jax.ShapeDtypeStruct((n_ctx, 2, HV, DK, DV), F32),
        ],
        compiler_params=_params(("arbitrary",)),
        name="delta_rule",
    )(*tables, q, k, v, bg, q, k, v, bg, state_delta)


def _dn_out_body(of_ref, ob_ref, z_ref, x_ref, mod_ref, nw_ref, wout_ref, n2_ref, x1_ref, h2_ref):
    o = of_ref[...].astype(F32) + ob_ref[...].astype(F32)
    z = z_ref[...].astype(F32)
    parts = []
    for h in range(HV):
        oh = o[:, h * DV:(h + 1) * DV]
        nh = oh * lax.rsqrt(jnp.mean(oh * oh, axis=-1, keepdims=True) + EPS) * nw_ref[...]
        parts.append((nh * _silu(z[:, h * DV:(h + 1) * DV])).astype(BF16))
    mix = jnp.dot(jnp.concatenate(parts, axis=1), wout_ref[...], preferred_element_type=F32)
    _moe_input(x_ref[...] + mod_ref[2:3, :] * mix, mod_ref, n2_ref, x1_ref, h2_ref)


def _dn_out(o_f, o_b, z, x, mod, nw, wout, n2, n_ctx_tiles, tiles_per_seq):
    n_tok = x.shape[0]
    tok, const, modm = _tile_maps(n_ctx_tiles, tiles_per_seq)
    out_specs, out_shapes = _mixer_outs(n_tok, tok)
    return pl.pallas_call(
        _dn_out_body,
        grid=(n_tok // TM,),
        in_specs=[
            pl.BlockSpec((TM, V_DIM), tok),
            pl.BlockSpec((TM, V_DIM), tok),
            pl.BlockSpec((TM, V_DIM), tok),
            pl.BlockSpec((TM, D), tok),
            pl.BlockSpec((None, 6, D), modm),
            pl.BlockSpec((1, DV), const),
            pl.BlockSpec((V_DIM, D), const),
            pl.BlockSpec((1, D), const),
        ],
        out_specs=out_specs,
        out_shape=out_shapes,
        compiler_params=_params(("arbitrary",)),
        name="dn_out",
    )(o_f, o_b, z, x, mod, nw, wout, n2)


def kernel(x_prompt, x_sample, state_delta, c, c_ctx, ada_w, ada_b, norm1_w, norm2_w, conv_in_w, conv_w,
           conv_out_w, dn_in_w, dn_conv_w, dn_a_log, dn_dt_bias, dn_norm_w, dn_out_w, router_w, router_b,
           exp_gu_w, exp_gu_b, exp_down_w, exp_down_b, final_norm_w):
    n_ctx, ctx_len, _ = x_prompt.shape
    n_dec, dec_len, _ = x_sample.shape
    assert ctx_len == CTX_ROW and dec_len % TM == 0 and n_dec + 1 <= N_COND
    n_ctx_tok = n_ctx * ctx_len
    n_ctx_tiles = n_ctx_tok // TM
    tiles_per_seq = dec_len // TM

    cond = jnp.zeros((N_COND, D), F32).at[0].set(c_ctx).at[1:1 + n_dec].set(c)
    mod = _ada(cond, ada_w, ada_b)

    def moe(l, x1, h2, final):
        return _moe(x1, h2, router_w[l].T, router_b[l][:, None], mod[l], final_norm_w[None, :], l, exp_gu_w,
                    exp_gu_b, exp_down_w, exp_down_b, n_ctx_tiles, tiles_per_seq, final)

    x1, h2 = _conv_layer(x_prompt.reshape(n_ctx_tok, D), x_sample.reshape(n_dec * dec_len, D), mod[0],
                         norm1_w[0][None, :], conv_in_w[0].astype(BF16), conv_w[0], conv_out_w[0].astype(BF16),
                         norm2_w[0][None, :], n_ctx_tiles, tiles_per_seq)
    x, = moe(0, x1, h2, False)

    w_in = dn_in_w[0]
    w_ba = jnp.zeros((D, LANES), F32).at[:, :4 * HV].set(w_in[:, CONV_CH + V_DIM:]).astype(BF16)
    lane_pad = lambda a: jnp.zeros((1, LANES), F32).at[0, 2 * HV:4 * HV].set(a.reshape(-1))
    q, k, v, z, bg = _dn_in(x, mod[1], norm1_w[1][None, :], w_in[:, :CONV_CH + V_DIM].astype(BF16), w_ba,
                            dn_conv_w[0], lane_pad(dn_a_log[0]), lane_pad(dn_dt_bias[0]), n_ctx_tiles,
                            tiles_per_seq)
    o_f, o_b, s_ctx = _delta(q, k, v, bg, state_delta, n_ctx, ctx_len // CHUNK, n_dec, dec_len // CHUNK)
    x1, h2 = _dn_out(o_f, o_b, z, x, mod[1], dn_norm_w[0][None, :], dn_out_w[0].astype(BF16),
                     norm2_w[1][None, :], n_ctx_tiles, tiles_per_seq)
    y_ctx, y_dec = moe(1, x1, h2, True)
    return y_ctx.reshape(n_ctx, ctx_len, D), y_dec.reshape(n_dec, dec_len, D), s_ctx[:, None]
```

```python
import functools

import jax
import jax.numpy as jnp
from jax import lax
from jax.experimental import pallas as pl
from jax.experimental.pallas import tpu as pltpu

F32 = jnp.float32
BF16 = jnp.bfloat16

D = 1024
TM = 256
CTX_ROW = 256
GRID_W = 64
HK, HV, DK, DV = 8, 16, 128, 128
QK_DIM = HK * DK
V_DIM = HV * DV
CONV_CH = 2 * QK_DIM + V_DIM
CHUNK = 64
N_EXPERTS = 32
TOP_K = 4
D_FF = 1024
DP = D // 2
SWIGLU_ALPHA = 1.702
SWIGLU_LIMIT = 7.0
EPS = 1e-6
LANES = 128
N_COND = 8
BLK = 512
NEG = -3.0e38
VMEM_LIMIT = 56 * 1024 * 1024
HIGHEST = lax.Precision.HIGHEST
G0 = 2 * HV
PAIR = HV // HK


def _silu(x):
    return x * jax.nn.sigmoid(x)


def _bdot(a, b):
    return jnp.dot(a.astype(BF16), b.astype(BF16), preferred_element_type=F32)


def _pack_bf16(x):
    m = x.shape[1] // 2

    def rne(v):
        bits = lax.bitcast_convert_type(v, jnp.uint32)
        return (bits + jnp.uint32(0x7FFF) + ((bits >> 16) & jnp.uint32(1))) >> 16

    return rne(x[:, :m]) | (rne(x[:, m:]) << 16)


def _pack_bf16_exact(x):
    m = x.shape[1] // 2
    bits = lax.bitcast_convert_type(x, jnp.uint32)
    return (bits[:, :m] >> 16) | bits[:, m:]


def _unpack_bf16(p):
    lo = lax.bitcast_convert_type(p << 16, F32)
    hi = lax.bitcast_convert_type(p & jnp.uint32(0xFFFF0000), F32)
    return jnp.concatenate([lo.astype(BF16), hi.astype(BF16)], axis=1)


def _params(sem):
    return pltpu.CompilerParams(dimension_semantics=sem, vmem_limit_bytes=VMEM_LIMIT)


def _ada_body(c_ref, w_ref, b_ref, o_ref):
    o_ref[...] = _bdot(_silu(c_ref[...]), w_ref[...]) + b_ref[...]


def _ada(cond, ada_w, ada_b):
    n_layers = ada_w.shape[0]
    tn = 1024
    out = pl.pallas_call(
        _ada_body,
        grid=(n_layers, 6 * D // tn),
        in_specs=[
            pl.BlockSpec((N_COND, D), lambda l, j: (0, 0)),
            pl.BlockSpec((None, D, tn), lambda l, j: (l, 0, j)),
            pl.BlockSpec((None, 1, tn), lambda l, j: (l, 0, j)),
        ],
        out_specs=pl.BlockSpec((None, N_COND, tn), lambda l, j: (l, 0, j)),
        out_shape=jax.ShapeDtypeStruct((n_layers, N_COND, 6 * D), F32),
        compiler_params=_params(("arbitrary", "arbitrary")),
        name="ada",
    )(cond, ada_w, ada_b.reshape(n_layers, 1, 6 * D))
    return out.reshape(n_layers, N_COND, 6, D)


def _prenorm(x, nw, sc, sh):
    ms = jnp.mean(x * x, axis=-1, keepdims=True)
    return (x * lax.rsqrt(ms + EPS) * nw) * (1.0 + sc) + sh


def _short_conv(v, w_ref, row_len):
    n = v.shape[0]
    pos = lax.broadcasted_iota(jnp.int32, (n, 1), 0) & (row_len - 1)
    prev = jnp.where(pos == 0, 0.0, pltpu.roll(v, 1, 0))
    nxt = jnp.where(pos == row_len - 1, 0.0, pltpu.roll(v, n - 1, 0))
    return prev * w_ref[0:1, :] + v * w_ref[1:2, :] + nxt * w_ref[2:3, :]


RT = 1024
TILES = 2


def _route_body(h_ref, rwt_ref, rb_ref, idx_ref, lrank_ref, gate_ref, cnt_ref):
    logits = lax.dot_general(rwt_ref[...], h_ref[...], (((1,), (1,)), ((), ())), preferred_element_type=F32,
                             precision=HIGHEST) + rb_ref[...]
    eid = lax.broadcasted_iota(jnp.int32, (N_EXPERTS, RT), 0)
    work = logits
    sel = jnp.zeros((N_EXPERTS, RT), F32)
    vals, ids, hots = [], [], []
    for _ in range(TOP_K):
        m = jnp.max(work, axis=0, keepdims=True)
        ik = jnp.min(jnp.where(work == m, eid, N_EXPERTS), axis=0, keepdims=True)
        hot = eid == ik
        work = jnp.where(hot, NEG, work)
        sel = sel + hot.astype(F32)
        vals.append(m)
        ids.append(ik)
        hots.append(hot)
    es = [jnp.exp(v - vals[0]) for v in vals]
    den = es[0] + es[1] + es[2] + es[3]
    earlier = (lax.broadcasted_iota(jnp.int32, (TM, TM), 0) < lax.broadcasted_iota(jnp.int32, (TM, TM), 1))
    earlier = earlier.astype(BF16)
    ranks, cnts = [], []
    for j in range(RT // TM):
        part = sel[:, j * TM:(j + 1) * TM]
        ranks.append(jnp.dot(part.astype(BF16), earlier, preferred_element_type=F32))
        cnts.append(jnp.broadcast_to(jnp.sum(part, axis=1, keepdims=True), (N_EXPERTS, TM)))
    rank_all = jnp.concatenate(ranks, axis=1)
    cnt_ref[...] = jnp.concatenate(cnts, axis=1).astype(jnp.int32)
    slot = lax.broadcasted_iota(jnp.int32, (8, RT), 0)
    idx_o = jnp.zeros((8, RT), jnp.int32)
    rank_o = jnp.zeros((8, RT), jnp.int32)
    gate_o = jnp.zeros((8, RT), F32)
    for k in range(TOP_K):
        rk = jnp.sum(jnp.where(hots[k], rank_all, 0.0), axis=0, keepdims=True)
        idx_o = jnp.where(slot == k, ids[k], idx_o)
        rank_o = jnp.where(slot == k, rk.astype(jnp.int32), rank_o)
        gate_o = jnp.where(slot == k, es[k] / den, gate_o)
    idx_ref[...] = idx_o
    lrank_ref[...] = rank_o
    gate_ref[...] = gate_o


def _route(h2, rwt, rb):
    n_tok = h2.shape[0]
    const = lambda i: (0, 0)
    cols = lambda i: (0, i)
    return pl.pallas_call(
        _route_body,
        grid=(n_tok // RT,),
        in_specs=[
            pl.BlockSpec((RT, D), lambda i: (i, 0)),
            pl.BlockSpec((N_EXPERTS, D), const),
            pl.BlockSpec((N_EXPERTS, 1), const),
        ],
        out_specs=[
            pl.BlockSpec((8, RT), cols),
            pl.BlockSpec((8, RT), cols),
            pl.BlockSpec((8, RT), cols),
            pl.BlockSpec((N_EXPERTS, RT), cols),
        ],
        out_shape=[
            jax.ShapeDtypeStruct((8, n_tok), jnp.int32),
            jax.ShapeDtypeStruct((8, n_tok), jnp.int32),
            jax.ShapeDtypeStruct((8, n_tok), F32),
            jax.ShapeDtypeStruct((N_EXPERTS, n_tok), jnp.int32),
        ],
        compiler_params=_params(("arbitrary",)),
        name="moe_route",
    )(h2, rwt, rb)


def _tile_maps(n_ctx_tiles, tiles_per_seq):
    def grp(i):
        return jnp.where(i < n_ctx_tiles, 0, 1 + (i - n_ctx_tiles) // tiles_per_seq)

    tok = lambda i: (i, 0)
    const = lambda i: (0, 0)
    mod = lambda i: (grp(i), 0, 0)
    return tok, const, mod


def _mixer_outs(n_tok, tok):
    return ([pl.BlockSpec((TILES * TM, D), tok), pl.BlockSpec((TILES * TM, D), tok)],
            [jax.ShapeDtypeStruct((n_tok, D), F32), jax.ShapeDtypeStruct((n_tok, D), F32)])


def _moe_input(x1, mod_ref, n2_ref, x1_ref, h2_ref):
    x1_ref[...] = x1
    h2_ref[...] = _prenorm(x1, n2_ref[...], mod_ref[4:5, :], mod_ref[3:4, :])


def _interleave(gens):
    live = list(gens)
    while live:
        live = [g for g in live if next(g, StopIteration) is not StopIteration]


def _tile_views(refs, g):
    return [r.at[pl.ds(g * TM, TM)] for r in refs]


def _conv_tile(from_ctx, row_len, xp_ref, xs_ref, mod_ref, n1_ref, win_ref, cw_ref, wout_ref, n2_ref, x1_ref, h2_ref):
    x = jnp.where(from_ctx, xp_ref[...], xs_ref[...])
    h = _prenorm(x, n1_ref[...], mod_ref[1:2, :], mod_ref[0:1, :]).astype(BF16)
    yield
    proj = jnp.dot(h, win_ref[...], preferred_element_type=F32)
    yield
    b, cg, u = proj[:, :D], proj[:, D:2 * D], proj[:, 2 * D:]
    gated = (b * _short_conv(cg * u, cw_ref, row_len)).astype(BF16)
    yield
    mix = jnp.dot(gated, wout_ref[...], preferred_element_type=F32)
    yield
    _moe_input(x + mod_ref[2:3, :] * mix, mod_ref, n2_ref, x1_ref, h2_ref)
    yield


def _conv_body(n_ctx_steps, xp_ref, xs_ref, mod_ref, n1_ref, win_ref, cw_ref, wout_ref, n2_ref, x1_ref, h2_ref):
    i = pl.program_id(0)
    from_ctx = i < n_ctx_steps
    row_len = jnp.where(from_ctx, CTX_ROW, GRID_W)
    tiles = []
    for g in range(TILES):
        xp, xs, x1, h2 = _tile_views([xp_ref, xs_ref, x1_ref, h2_ref], g)
        tiles.append(_conv_tile(from_ctx, row_len, xp, xs, mod_ref, n1_ref, win_ref, cw_ref, wout_ref, n2_ref, x1, h2))
    _interleave(tiles)


def _mixer_grid(n_tok, n_ctx_tiles, tiles_per_seq):
    assert n_ctx_tiles % TILES == 0 and tiles_per_seq % TILES == 0
    return n_tok // (TILES * TM), n_ctx_tiles // TILES, tiles_per_seq // TILES


def _conv_layer(x_ctx, x_dec, mod, n1, win, cw, wout, n2, n_ctx_tiles, tiles_per_seq):
    n_tok = x_ctx.shape[0] + x_dec.shape[0]
    n_steps, n_ctx_steps, steps_per_seq = _mixer_grid(n_tok, n_ctx_tiles, tiles_per_seq)
    tok, const, modm = _tile_maps(n_ctx_steps, steps_per_seq)
    out_specs, out_shapes = _mixer_outs(n_tok, tok)
    return pl.pallas_call(
        functools.partial(_conv_body, n_ctx_steps),
        grid=(n_steps,),
        in_specs=[
            pl.BlockSpec((TILES * TM, D), lambda i: (jnp.minimum(i, n_ctx_steps - 1), 0)),
            pl.BlockSpec((TILES * TM, D), lambda i: (jnp.maximum(i - n_ctx_steps, 0), 0)),
            pl.BlockSpec((None, 6, D), modm),
            pl.BlockSpec((1, D), const),
            pl.BlockSpec((D, 3 * D), const),
            pl.BlockSpec((3, D), const),
            pl.BlockSpec((D, D), const),
            pl.BlockSpec((1, D), const),
        ],
        out_specs=out_specs,
        out_shape=out_shapes,
        compiler_params=_params(("arbitrary",)),
        name="conv_layer",
    )(x_ctx, x_dec, mod, n1, win, cw, wout, n2)


RUN_ALIGN = 8
RUN_SIZES = (256, 128, 64, 32, 16, 8)
SORT_ROWS = TM * TOP_K + N_EXPERTS * RUN_ALIGN


def _run_pieces(src_ref, dst_ref, len_ref, tile):
    for e in range(N_EXPERTS):
        n = len_ref[tile * N_EXPERTS + e]
        src = src_ref[tile * N_EXPERTS + e]
        dst = dst_ref[tile * N_EXPERTS + e]
        for size in RUN_SIZES:
            done = n & ~(2 * size - 1)
            yield ((n & size) != 0, pl.multiple_of(src + done, RUN_ALIGN), pl.multiple_of(dst + done, RUN_ALIGN),
                   size)


def _sorted_onehot(lpos, rows_major):
    if rows_major:
        j = lax.broadcasted_iota(jnp.int32, (SORT_ROWS, TM), 0)
        return [j == lpos[k:k + 1, :] for k in range(TOP_K)]
    j = lax.broadcasted_iota(jnp.int32, (TM, SORT_ROWS), 1)
    return [j == lpos[:, k:k + 1] for k in range(TOP_K)]


def _scatter_body(src_ref, dst_ref, len_ref, pend_ref, lpos_ref, h_ref, xs_ref, sorted_ref, zbuf, sem, zsem):
    i = pl.program_id(0)

    @pl.when(i == 0)
    def _():
        zbuf[...] = jnp.zeros_like(zbuf)

        def zero_block(j):
            return pltpu.make_async_copy(zbuf, xs_ref.at[pl.ds(pl.multiple_of(j * BLK, BLK), BLK)], zsem)

        def start(j, carry):
            zero_block(j).start()
            return carry

        def wait(j, carry):
            zero_block(j).wait()
            return carry

        n_used = pend_ref[N_EXPERTS - 1] // BLK
        lax.fori_loop(n_used, xs_ref.shape[0] // BLK, start, 0)
        lax.fori_loop(n_used, xs_ref.shape[0] // BLK, wait, 0)
        for phase in range(2):
            for e in range(N_EXPERTS):
                end = pend_ref[e]
                begin = pend_ref[e - 1] if e else 0

                @pl.when(end > begin)
                def _():
                    tail = pl.ds(pl.multiple_of(end - BLK, BLK), BLK)
                    cp = pltpu.make_async_copy(zbuf, xs_ref.at[tail], zsem)
                    if phase == 0:
                        cp.start()
                    else:
                        cp.wait()

    hots = _sorted_onehot(lpos_ref[...], True)
    onehot = (hots[0] | hots[1] | hots[2] | hots[3]).astype(BF16)
    sorted_ref[...] = _pack_bf16_exact(jnp.dot(onehot, h_ref[...].astype(BF16), preferred_element_type=F32))
    for phase in range(2):
        for cond, src, dst, size in _run_pieces(src_ref, dst_ref, len_ref, i):
            @pl.when(cond)
            def _():
                cp = pltpu.make_async_copy(sorted_ref.at[pl.ds(src, size)], xs_ref.at[pl.ds(dst, size)], sem)
                if phase == 0:
                    cp.start()
                else:
                    cp.wait()


def _scatter(src_start, dst_start, run_len, pend, lpos, h2, n_slots):
    n_tok = h2.shape[0]
    return pl.pallas_call(
        _scatter_body,
        grid_spec=pltpu.PrefetchScalarGridSpec(
            num_scalar_prefetch=4,
            grid=(n_tok // TM,),
            in_specs=[pl.BlockSpec((8, TM), lambda i, *_: (0, i)), pl.BlockSpec((TM, D), lambda i, *_: (i, 0))],
            out_specs=pl.BlockSpec(memory_space=pl.ANY),
            scratch_shapes=[pltpu.VMEM((SORT_ROWS, DP), jnp.uint32), pltpu.VMEM((BLK, DP), jnp.uint32),
                            pltpu.SemaphoreType.DMA, pltpu.SemaphoreType.DMA],
        ),
        out_shape=jax.ShapeDtypeStruct((n_slots, DP), jnp.uint32),
        compiler_params=_params(("arbitrary",)),
        name="moe_scatter",
    )(src_start, dst_start, run_len, pend, lpos, h2)


W_ROWS = 128


def _expert_body(be_ref, nu_ref, x_ref, wgu_ref, bgu_ref, wd_ref, bd_ref, y_ref, wgu_bf, wd_bf):
    j = pl.program_id(0)
    n_used = nu_ref[0]
    jj = jnp.minimum(j, n_used - 1)
    e = be_ref[jj]
    e_prev = be_ref[jnp.maximum(jj - 1, 0)]

    @pl.when((j == 0) | (e != e_prev))
    def _():
        def cast(r, carry):
            rows = pl.ds(pl.multiple_of(r * W_ROWS, W_ROWS), W_ROWS)
            wgu_bf[rows, :] = wgu_ref[rows, :].astype(BF16)
            wd_bf[rows, :] = wd_ref[rows, :].astype(BF16)
            return carry

        lax.fori_loop(0, D // W_ROWS, cast, 0)

    @pl.when(j < n_used)
    def _():
        hh = jnp.dot(_unpack_bf16(x_ref[...]), wgu_bf[...], preferred_element_type=F32) + bgu_ref[...]
        hg = jnp.minimum(hh[:, :D_FF], SWIGLU_LIMIT)
        hl = jnp.clip(hh[:, D_FF:], -SWIGLU_LIMIT, SWIGLU_LIMIT)
        act = hg * jax.nn.sigmoid(SWIGLU_ALPHA * hg) * (hl + 1.0)
        y_ref[...] = _pack_bf16(jnp.dot(act.astype(BF16), wd_bf[...], preferred_element_type=F32) + bd_ref[...])

    @pl.when(j >= n_used)
    def _():
        y_ref[...] = jnp.zeros_like(y_ref)


def _experts(block_e, n_used, xs, layer, wgu, bgu, wd, bd):
    n_slots = xs.shape[0]
    n_layers = wgu.shape[0]

    def blk(j, be, nu):
        return (jnp.minimum(j, nu[0] - 1), 0)

    def wmap(j, be, nu):
        return (layer, be[jnp.minimum(j, nu[0] - 1)], 0, 0)

    return pl.pallas_call(
        _expert_body,
        grid_spec=pltpu.PrefetchScalarGridSpec(
            num_scalar_prefetch=2,
            grid=(n_slots // BLK,),
            in_specs=[
                pl.BlockSpec((BLK, DP), blk),
                pl.BlockSpec((None, None, D, 2 * D_FF), wmap),
                pl.BlockSpec((None, None, 1, 2 * D_FF), wmap),
                pl.BlockSpec((None, None, D_FF, D), wmap),
                pl.BlockSpec((None, None, 1, D), wmap),
            ],
            out_specs=pl.BlockSpec((BLK, DP), lambda j, be, nu: (j, 0)),
            scratch_shapes=[pltpu.VMEM((D, 2 * D_FF), BF16), pltpu.VMEM((D_FF, D), BF16)],
        ),
        out_shape=jax.ShapeDtypeStruct((n_slots, DP), jnp.uint32),
        compiler_params=_params(("arbitrary",)),
        name="moe_experts",
    )(block_e, n_used, xs, wgu, bgu.reshape(n_layers, N_EXPERTS, 1, 2 * D_FF), wd,
      bd.reshape(n_layers, N_EXPERTS, 1, D))


def _combine_body(n_ctx_tiles, src_ref, dst_ref, len_ref, x_ref, lpos_ref, gate_ref, mod_ref, fnw_ref, y_ref, *rest):
    *o_refs, buf, sem = rest
    i = pl.program_id(0)
    n_tiles = pl.num_programs(0)

    def runs(tile, slot, start):
        for cond, src, dst, size in _run_pieces(src_ref, dst_ref, len_ref, tile):
            @pl.when(cond)
            def _():
                cp = pltpu.make_async_copy(y_ref.at[pl.ds(dst, size)], buf.at[slot, pl.ds(src, size)], sem.at[slot])
                if start:
                    cp.start()
                else:
                    cp.wait()

    @pl.when(i == 0)
    def _():
        buf[...] = jnp.zeros_like(buf)
        runs(0, 0, True)

    @pl.when(i + 1 < n_tiles)
    def _():
        runs(i + 1, (i + 1) % 2, True)

    slot = i % 2
    runs(i, slot, False)
    gate = gate_ref[...].T
    hots = _sorted_onehot(lpos_ref[...].T, False)
    placed = jnp.where(hots[0], gate[:, 0:1], 0.0)
    for k in range(1, TOP_K):
        placed = jnp.where(hots[k], gate[:, k:k + 1], placed)
    acc = jnp.dot(placed.astype(BF16), _unpack_bf16(buf[slot]), preferred_element_type=F32)
    x2 = x_ref[...] + mod_ref[5:6, :] * acc
    if n_ctx_tiles is None:
        o_refs[0][...] = x2
    else:
        ms = jnp.mean(x2 * x2, axis=-1, keepdims=True)
        y = x2 * lax.rsqrt(ms + EPS) * fnw_ref[...]

        @pl.when(i < n_ctx_tiles)
        def _():
            o_refs[0][...] = y

        @pl.when(i >= n_ctx_tiles)
        def _():
            o_refs[1][...] = y


def _combine(src_start, dst_start, run_len, x1, lpos, gate, mod, fnw, y_slots, n_ctx_tiles, tiles_per_seq, final):
    n_tok = x1.shape[0]

    def grp(i):
        return jnp.where(i < n_ctx_tiles, 0, 1 + (i - n_ctx_tiles) // tiles_per_seq)

    if final:
        out_specs = [pl.BlockSpec((TM, D), lambda i, *_: (jnp.minimum(i, n_ctx_tiles - 1), 0)),
                     pl.BlockSpec((TM, D), lambda i, *_: (jnp.maximum(i - n_ctx_tiles, 0), 0))]
        out_shape = [jax.ShapeDtypeStruct((n_ctx_tiles * TM, D), F32),
                     jax.ShapeDtypeStruct((n_tok - n_ctx_tiles * TM, D), F32)]
    else:
        out_specs = [pl.BlockSpec((TM, D), lambda i, *_: (i, 0))]
        out_shape = [jax.ShapeDtypeStruct((n_tok, D), F32)]
    return pl.pallas_call(
        functools.partial(_combine_body, n_ctx_tiles if final else None),
        grid_spec=pltpu.PrefetchScalarGridSpec(
            num_scalar_prefetch=3,
            grid=(n_tok // TM,),
            in_specs=[
                pl.BlockSpec((TM, D), lambda i, *_: (i, 0)),
                pl.BlockSpec((8, TM), lambda i, *_: (0, i)),
                pl.BlockSpec((8, TM), lambda i, *_: (0, i)),
                pl.BlockSpec((None, 6, D), lambda i, *_: (grp(i), 0, 0)),
                pl.BlockSpec((1, D), lambda i, *_: (0, 0)),
                pl.BlockSpec(memory_space=pl.ANY),
            ],
            out_specs=out_specs,
            scratch_shapes=[pltpu.VMEM((2, SORT_ROWS, DP), jnp.uint32), pltpu.SemaphoreType.DMA((2,))],
        ),
        out_shape=out_shape,
        compiler_params=_params(("arbitrary",)),
        name="moe_combine",
    )(src_start, dst_start, run_len, x1, lpos, gate, mod, fnw, y_slots)


def _moe(x1, h2, rwt, rb, mod, fnw, layer, wgu, bgu, wd, bd, n_ctx_tiles, tiles_per_seq, final):
    n_tok = x1.shape[0]
    n_tiles = n_tok // TM
    idx, lrank, gate, cnt_tok = _route(h2, rwt, rb)
    cnt = cnt_tok[:, ::TM].T
    run_len = (cnt + RUN_ALIGN - 1) // RUN_ALIGN * RUN_ALIGN
    src_start = jnp.cumsum(run_len, axis=1) - run_len
    total = jnp.sum(run_len, axis=0)
    padded = (total + BLK - 1) // BLK * BLK
    pend = jnp.cumsum(padded)
    dst_start = (pend - padded)[None, :] + jnp.cumsum(run_len, axis=0) - run_len
    hit = idx[:TOP_K, None, :] == jnp.arange(N_EXPERTS, dtype=jnp.int32)[None, :, None]
    src_tok = jnp.repeat(src_start.T, TM, axis=1)
    lpos = jnp.sum(jnp.where(hit, src_tok[None], 0), axis=1) + lrank[:TOP_K]
    lpos = jnp.concatenate([lpos, jnp.full((8 - TOP_K, n_tok), -1, jnp.int32)], axis=0).astype(jnp.int32)
    n_slots = (n_tok * TOP_K + n_tiles * N_EXPERTS * (RUN_ALIGN - 1)) // BLK * BLK + (N_EXPERTS + 1) * BLK
    n_blocks = n_slots // BLK
    block_start = jnp.arange(n_blocks, dtype=jnp.int32)[:, None] * BLK
    block_e = jnp.minimum(jnp.sum(pend[None, :] <= block_start, axis=1), N_EXPERTS - 1).astype(jnp.int32)
    n_used = (pend[-1:] // BLK).astype(jnp.int32)
    tables = [t.reshape(-1).astype(jnp.int32) for t in (src_start, dst_start, run_len)]
    xs = _scatter(*tables, pend.astype(jnp.int32), lpos, h2, n_slots)
    ys = _experts(block_e, n_used, xs, layer, wgu, bgu, wd, bd)
    return _combine(*tables, x1, lpos, gate, mod, fnw, ys, n_ctx_tiles, tiles_per_seq, final)


def _dn_in_tile(row_len, x_ref, mod_ref, n1_ref, w_ref, wba_ref, cw_ref, alog_ref, dtb_ref,
                q_ref, k_ref, v_ref, z_ref, bg_ref):
    h = _prenorm(x_ref[...], n1_ref[...], mod_ref[1:2, :], mod_ref[0:1, :]).astype(BF16)
    yield
    proj = jnp.dot(h, w_ref[...], preferred_element_type=F32)
    yield
    z_ref[...] = proj[:, CONV_CH:].astype(BF16)
    qkv = _silu(_short_conv(proj[:, :CONV_CH], cw_ref, row_len))
    yield
    for hd in range(2 * HK):
        s = qkv[:, hd * DK:(hd + 1) * DK]
        n = s * lax.rsqrt(jnp.sum(s * s, axis=-1, keepdims=True) + EPS)
        if hd < HK:
            q_ref[:, hd * DK:(hd + 1) * DK] = n * (DK ** -0.5)
        else:
            k_ref[:, (hd - HK) * DK:(hd - HK + 1) * DK] = n
    v_ref[...] = qkv[:, 2 * QK_DIM:]
    yield
    ba = jnp.dot(h, wba_ref[...], preferred_element_type=F32)
    beta = jax.nn.sigmoid(ba)
    a = ba + dtb_ref[...]
    softplus = jnp.maximum(a, 0.0) + jnp.log(1.0 + jnp.exp(-jnp.abs(a)))
    g = -jnp.exp(alog_ref[...]) * softplus
    lane = lax.broadcasted_iota(jnp.int32, (TM, LANES), 1)
    bg_ref[...] = jnp.where(lane < G0, beta, g)
    yield


def _dn_in_body(n_ctx_steps, x_ref, mod_ref, n1_ref, w_ref, wba_ref, cw_ref, alog_ref, dtb_ref,
                q_ref, k_ref, v_ref, z_ref, bg_ref):
    row_len = jnp.where(pl.program_id(0) < n_ctx_steps, CTX_ROW, GRID_W)
    tiles = []
    for g in range(TILES):
        x, q, k, v, z, bg = _tile_views([x_ref, q_ref, k_ref, v_ref, z_ref, bg_ref], g)
        tiles.append(_dn_in_tile(row_len, x, mod_ref, n1_ref, w_ref, wba_ref, cw_ref, alog_ref, dtb_ref, q, k, v, z, bg))
    _interleave(tiles)


def _dn_in(x, mod, n1, w, wba, cw, alog, dtb, n_ctx_tiles, tiles_per_seq):
    n_tok = x.shape[0]
    n_steps, n_ctx_steps, steps_per_seq = _mixer_grid(n_tok, n_ctx_tiles, tiles_per_seq)
    tok, const, modm = _tile_maps(n_ctx_steps, steps_per_seq)
    rows = TILES * TM
    return pl.pallas_call(
        functools.partial(_dn_in_body, n_ctx_steps),
        grid=(n_steps,),
        in_specs=[
            pl.BlockSpec((rows, D), tok),
            pl.BlockSpec((None, 6, D), modm),
            pl.BlockSpec((1, D), const),
            pl.BlockSpec((D, CONV_CH + V_DIM), const),
            pl.BlockSpec((D, LANES), const),
            pl.BlockSpec((3, CONV_CH), const),
            pl.BlockSpec((1, LANES), const),
            pl.BlockSpec((1, LANES), const),
        ],
        out_specs=[
            pl.BlockSpec((rows, QK_DIM), tok),
            pl.BlockSpec((rows, QK_DIM), tok),
            pl.BlockSpec((rows, V_DIM), tok),
            pl.BlockSpec((rows, V_DIM), tok),
            pl.BlockSpec((rows, LANES), tok),
        ],
        out_shape=[
            jax.ShapeDtypeStruct((n_tok, QK_DIM), F32),
            jax.ShapeDtypeStruct((n_tok, QK_DIM), F32),
            jax.ShapeDtypeStruct((n_tok, V_DIM), F32),
            jax.ShapeDtypeStruct((n_tok, V_DIM), BF16),
            jax.ShapeDtypeStruct((n_tok, LANES), F32),
        ],
        compiler_params=_params(("arbitrary",)),
        name="dn_in",
    )(x, mod, n1, w, wba, cw, alog, dtb)


def _hdot(a, b):
    return jnp.dot(a, b, preferred_element_type=F32, precision=HIGHEST)


def _nt_dot(a, b):
    return lax.dot_general(a, b, (((1,), (1,)), ((), ())), preferred_element_type=F32)


def _delta_dir(bwd, q_ref, k_ref, v_ref, bg_ref, o_ref, state):
    r2 = lax.broadcasted_iota(jnp.int32, (CHUNK, LANES), 0)
    lane = lax.broadcasted_iota(jnp.int32, (CHUNK, LANES), 1)
    c2 = lane & (CHUNK - 1)
    left = lane < CHUNK
    causal2 = (r2 <= c2) if bwd else (r2 >= c2)
    strict2 = (r2 < c2) if bwd else (r2 > c2)
    eye2 = (r2 == c2).astype(F32)
    lvl = r2 ^ c2
    b0 = HV if bwd else 0
    g0 = 2 * HV + b0
    bg = bg_ref[...]
    gc = _hdot(causal2[:, :CHUNK].astype(F32), bg)
    gct = gc.T
    gtot = jnp.sum(bg, axis=0, keepdims=True)
    egc = jnp.exp(gc)
    eend = jnp.exp(gtot - gc)
    etot = jnp.exp(gtot)

    def pair_cols(a, j):
        return jnp.where(left, a[:, j:j + 1], a[:, j + 1:j + 2])

    def bdiag(x):
        z = jnp.zeros_like(x)
        return jnp.concatenate([jnp.where(left, x, z), jnp.where(left, z, x)], axis=0).astype(BF16)

    eye_k = (lax.broadcasted_iota(jnp.int32, (DK, DK), 0) == lax.broadcasted_iota(jnp.int32, (DK, DK), 1)).astype(BF16)
    ns, attns, kts = [], [], []
    for p in range(HK):
        j = g0 + PAIR * p
        k16 = k_ref[:, p * DK:(p + 1) * DK].astype(BF16)
        q16 = q_ref[:, p * DK:(p + 1) * DK].astype(BF16)
        kq = _nt_dot(jnp.concatenate([k16, q16], axis=0), jnp.concatenate([k16, k16], axis=0))
        grow2 = jnp.concatenate([gct[j:j + 1, :], gct[j + 1:j + 2, :]], axis=1)
        decay2 = jnp.exp(jnp.where(causal2, pair_cols(gc, j) - grow2, NEG))
        ns.append(jnp.where(strict2, kq[:CHUNK] * decay2 * pair_cols(bg, b0 + PAIR * p), 0.0))
        attns.append(kq[CHUNK:] * decay2)
        kts.append(_nt_dot(eye_k, k16).astype(BF16))
        yield

    def mdot(a, b):
        return jnp.dot(a.astype(BF16), bdiag(b), preferred_element_type=F32)

    n4 = [jnp.where(lvl < 4, n, 0.0) for n in ns]
    sq = [mdot(a, a) for a in n4]
    yield
    ts = [eye2 - a for a in n4]
    ts = [t + mdot(t, s) for t, s in zip(ts, sq)]
    yield
    for bit in range(2, 6):
        ys = [mdot(t, jnp.where((lvl >> bit) == 1, n, 0.0)) for t, n in zip(ts, ns)]
        yield
        ts = [t - mdot(y, t) for t, y in zip(ts, ys)]
        yield

    sols = []
    for p in range(HK):
        kh = k_ref[:, p * DK:(p + 1) * DK]
        rhs = []
        for e in range(PAIR):
            h = PAIR * p + e
            beta = bg[:, b0 + h:b0 + h + 1]
            rhs.append(jnp.concatenate(
                [v_ref[:, h * DV:(h + 1) * DV] * beta, kh * (beta * egc[:, g0 + h:g0 + h + 1])], axis=1))
        sols.append(jnp.dot(bdiag(ts[p]), jnp.concatenate(rhs, axis=0).astype(BF16), preferred_element_type=F32))
        yield

    outs = []
    for h in range(HV):
        p, e = divmod(h, PAIR)
        qg = q_ref[:, p * DK:(p + 1) * DK] * egc[:, g0 + h:g0 + h + 1]
        lhs = jnp.concatenate([sols[p][e * CHUNK:(e + 1) * CHUNK, DV:], qg], axis=0)
        outs.append(_bdot(lhs, state[bwd, h]))
        yield

    for p in range(HK):
        h0 = PAIR * p
        vns = [sols[p][e * CHUNK:(e + 1) * CHUNK, :DV] - outs[h0 + e][:CHUNK] for e in range(PAIR)]
        intra = jnp.dot(bdiag(attns[p]), jnp.concatenate(vns, axis=0).astype(BF16), preferred_element_type=F32)
        for e in range(PAIR):
            o_ref[:, (h0 + e) * DV:(h0 + e + 1) * DV] = (outs[h0 + e][CHUNK:] + intra[e * CHUNK:(e + 1) * CHUNK]).astype(BF16)
        scaled = jnp.concatenate([vns[e] * eend[:, g0 + h0 + e:g0 + h0 + e + 1] for e in range(PAIR)], axis=1)
        kv = jnp.dot(kts[p], scaled.astype(BF16), preferred_element_type=F32)
        for e in range(PAIR):
            h = h0 + e
            state[bwd, h] = state[bwd, h] * etot[:, g0 + h:g0 + h + 1] + kv[:, e * DV:(e + 1) * DV]
        yield


FIRST, LAST, DEC = 1, 2, 4


def _delta_body(fb_ref, bb_ref, si_ref, oi_ref, fl_ref, qf_ref, kf_ref, vf_ref, bgf_ref, qb_ref, kb_ref, vb_ref,
                bgb_ref, s0_ref, of_ref, ob_ref, sout_ref, state):
    flags = fl_ref[pl.program_id(0)]

    @pl.when((flags & FIRST) != 0)
    def _():
        state[...] = jnp.where((flags & DEC) != 0, s0_ref[...], 0.0)

    _interleave([_delta_dir(0, qf_ref, kf_ref, vf_ref, bgf_ref, of_ref, state),
                 _delta_dir(1, qb_ref, kb_ref, vb_ref, bgb_ref, ob_ref, state)])

    @pl.when((flags & (LAST | DEC)) == LAST)
    def _():
        sout_ref[...] = state[...]


def _delta(q, k, v, bg, state_delta, n_ctx, ctx_chunks, n_dec, dec_chunks):
    n_tok = q.shape[0]
    fb, bb, si, oi, fl = [], [], [], [], []
    for dec, n_seq, n, base in ((0, n_ctx, ctx_chunks, 0), (1, n_dec, dec_chunks, n_ctx * ctx_chunks)):
        for b in range(n_seq):
            for c in range(n):
                fb.append(base + b * n + c)
                bb.append(base + b * n + n - 1 - c)
                si.append(b if dec else 0)
                oi.append(n_ctx - 1 if dec else b)
                fl.append((FIRST if c == 0 else 0) | (LAST if c == n - 1 else 0) | (DEC if dec else 0))
    tables = [jnp.asarray(t, jnp.int32) for t in (fb, bb, si, oi, fl)]
    fwd = lambda s, fb, bb, si, oi, fl: (fb[s], 0)
    bwd = lambda s, fb, bb, si, oi, fl: (bb[s], 0)
    tok_specs = lambda m: [pl.BlockSpec((CHUNK, QK_DIM), m), pl.BlockSpec((CHUNK, QK_DIM), m),
                           pl.BlockSpec((CHUNK, V_DIM), m), pl.BlockSpec((CHUNK, LANES), m)]
    return pl.pallas_call(
        _delta_body,
        grid_spec=pltpu.PrefetchScalarGridSpec(
            num_scalar_prefetch=5,
            grid=(len(fb),),
            in_specs=tok_specs(fwd) + tok_specs(bwd) + [
                pl.BlockSpec((None, None, 2, HV, DK, DV), lambda s, fb, bb, si, oi, fl: (si[s], 0, 0, 0, 0, 0)),
            ],
            out_specs=[
                pl.BlockSpec((CHUNK, V_DIM), fwd),
                pl.BlockSpec((CHUNK, V_DIM), bwd),
                pl.BlockSpec((None, 2, HV, DK, DV), lambda s, fb, bb, si, oi, fl: (oi[s], 0, 0, 0, 0)),
            ],
            scratch_shapes=[pltpu.VMEM((2, HV, DK, DV), F32)],
        ),
        out_shape=[
            jax.ShapeDtypeStruct((n_tok, V_DIM), BF16),
            jax.ShapeDtypeStruct((n_tok, V_DIM), BF16),
            jax.ShapeDtypeStruct((n_ctx, 2, HV, DK, DV), F32),
        ],
        compiler_params=_params(("arbitrary",)),
        name="delta_rule",
    )(*tables, q, k, v, bg, q, k, v, bg, state_delta)


def _dn_out_tile(of_ref, ob_ref, z_ref, x_ref, mod_ref, nw_ref, wout_ref, n2_ref, x1_ref, h2_ref):
    o = of_ref[...].astype(F32) + ob_ref[...].astype(F32)
    z = z_ref[...].astype(F32)
    parts = []
    for h in range(HV):
        oh = o[:, h * DV:(h + 1) * DV]
        nh = oh * lax.rsqrt(jnp.mean(oh * oh, axis=-1, keepdims=True) + EPS) * nw_ref[...]
        parts.append((nh * _silu(z[:, h * DV:(h + 1) * DV])).astype(BF16))
    gated = jnp.concatenate(parts, axis=1)
    yield
    mix = jnp.dot(gated, wout_ref[...], preferred_element_type=F32)
    yield
    _moe_input(x_ref[...] + mod_ref[2:3, :] * mix, mod_ref, n2_ref, x1_ref, h2_ref)
    yield


def _dn_out_body(of_ref, ob_ref, z_ref, x_ref, mod_ref, nw_ref, wout_ref, n2_ref, x1_ref, h2_ref):
    tiles = []
    for g in range(TILES):
        of, ob, z, x, x1, h2 = _tile_views([of_ref, ob_ref, z_ref, x_ref, x1_ref, h2_ref], g)
        tiles.append(_dn_out_tile(of, ob, z, x, mod_ref, nw_ref, wout_ref, n2_ref, x1, h2))
    _interleave(tiles)


def _dn_out(o_f, o_b, z, x, mod, nw, wout, n2, n_ctx_tiles, tiles_per_seq):
    n_tok = x.shape[0]
    n_steps, n_ctx_steps, steps_per_seq = _mixer_grid(n_tok, n_ctx_tiles, tiles_per_seq)
    tok, const, modm = _tile_maps(n_ctx_steps, steps_per_seq)
    out_specs, out_shapes = _mixer_outs(n_tok, tok)
    rows = TILES * TM
    return pl.pallas_call(
        _dn_out_body,
        grid=(n_steps,),
        in_specs=[
            pl.BlockSpec((rows, V_DIM), tok),
            pl.BlockSpec((rows, V_DIM), tok),
            pl.BlockSpec((rows, V_DIM), tok),
            pl.BlockSpec((rows, D), tok),
            pl.BlockSpec((None, 6, D), modm),
            pl.BlockSpec((1, DV), const),
            pl.BlockSpec((V_DIM, D), const),
            pl.BlockSpec((1, D), const),
        ],
        out_specs=out_specs,
        out_shape=out_shapes,
        compiler_params=_params(("arbitrary",)),
        name="dn_out",
    )(o_f, o_b, z, x, mod, nw, wout, n2)


def kernel(x_prompt, x_sample, state_delta, c, c_ctx, ada_w, ada_b, norm1_w, norm2_w, conv_in_w, conv_w,
           conv_out_w, dn_in_w, dn_conv_w, dn_a_log, dn_dt_bias, dn_norm_w, dn_out_w, router_w, router_b,
           exp_gu_w, exp_gu_b, exp_down_w, exp_down_b, final_norm_w):
    n_ctx, ctx_len, _ = x_prompt.shape
    n_dec, dec_len, _ = x_sample.shape
    assert ctx_len == CTX_ROW and dec_len % TM == 0 and n_dec + 1 <= N_COND
    n_ctx_tok = n_ctx * ctx_len
    n_ctx_tiles = n_ctx_tok // TM
    tiles_per_seq = dec_len // TM

    cond = jnp.zeros((N_COND, D), F32).at[0].set(c_ctx).at[1:1 + n_dec].set(c)
    mod = _ada(cond, ada_w, ada_b)

    def moe(l, x1, h2, final):
        return _moe(x1, h2, router_w[l].T, router_b[l][:, None], mod[l], final_norm_w[None, :], l, exp_gu_w,
                    exp_gu_b, exp_down_w, exp_down_b, n_ctx_tiles, tiles_per_seq, final)

    x1, h2 = _conv_layer(x_prompt.reshape(n_ctx_tok, D), x_sample.reshape(n_dec * dec_len, D), mod[0],
                         norm1_w[0][None, :], conv_in_w[0].astype(BF16), conv_w[0], conv_out_w[0].astype(BF16),
                         norm2_w[0][None, :], n_ctx_tiles, tiles_per_seq)
    x, = moe(0, x1, h2, False)

    w_in = dn_in_w[0]
    w_ba = jnp.zeros((D, LANES), F32).at[:, :4 * HV].set(w_in[:, CONV_CH + V_DIM:]).astype(BF16)
    lane_pad = lambda a: jnp.zeros((1, LANES), F32).at[0, 2 * HV:4 * HV].set(a.reshape(-1))
    q, k, v, z, bg = _dn_in(x, mod[1], norm1_w[1][None, :], w_in[:, :CONV_CH + V_DIM].astype(BF16), w_ba,
                            dn_conv_w[0], lane_pad(dn_a_log[0]), lane_pad(dn_dt_bias[0]), n_ctx_tiles,
                            tiles_per_seq)
    o_f, o_b, s_ctx = _delta(q, k, v, bg, state_delta, n_ctx, ctx_len // CHUNK, n_dec, dec_len // CHUNK)
    x1, h2 = _dn_out(o_f, o_b, z, x, mod[1], dn_norm_w[0][None, :], dn_out_w[0].astype(BF16),
                     norm2_w[1][None, :], n_ctx_tiles, tiles_per_seq)
    y_ctx, y_dec = moe(1, x1, h2, True)
    return y_ctx.reshape(n_ctx, ctx_len, D), y_dec.reshape(n_dec, dec_len, D), s_ctx[:, None]
```

```python
import functools

import jax
import jax.numpy as jnp
from jax import lax
from jax.experimental import pallas as pl
from jax.experimental.pallas import tpu as pltpu

F32 = jnp.float32
BF16 = jnp.bfloat16

D = 1024
TM = 256
CTX_ROW = 256
GRID_W = 64
HK, HV, DK, DV = 8, 16, 128, 128
QK_DIM = HK * DK
V_DIM = HV * DV
CONV_CH = 2 * QK_DIM + V_DIM
CHUNK = 64
N_EXPERTS = 32
TOP_K = 4
D_FF = 1024
DP = D // 2
SWIGLU_ALPHA = 1.702
SWIGLU_LIMIT = 7.0
EPS = 1e-6
LANES = 128
N_COND = 8
BLK = 512
NEG = -3.0e38
VMEM_LIMIT = 56 * 1024 * 1024
HIGHEST = lax.Precision.HIGHEST
G0 = 2 * HV
PAIR = HV // HK


def _silu(x):
    return x * jax.nn.sigmoid(x)


def _bdot(a, b):
    return jnp.dot(a.astype(BF16), b.astype(BF16), preferred_element_type=F32)


def _pack_bf16(x):
    m = x.shape[1] // 2

    def rne(v):
        bits = lax.bitcast_convert_type(v, jnp.uint32)
        return (bits + jnp.uint32(0x7FFF) + ((bits >> 16) & jnp.uint32(1))) >> 16

    return rne(x[:, :m]) | (rne(x[:, m:]) << 16)


def _pack_bf16_exact(x):
    m = x.shape[1] // 2
    bits = lax.bitcast_convert_type(x, jnp.uint32)
    return (bits[:, :m] >> 16) | bits[:, m:]


def _unpack_bf16(p):
    lo = lax.bitcast_convert_type(p << 16, F32)
    hi = lax.bitcast_convert_type(p & jnp.uint32(0xFFFF0000), F32)
    return jnp.concatenate([lo.astype(BF16), hi.astype(BF16)], axis=1)


def _params(sem):
    return pltpu.CompilerParams(dimension_semantics=sem, vmem_limit_bytes=VMEM_LIMIT)


def _ada_body(c_ref, w_ref, b_ref, o_ref):
    o_ref[...] = _bdot(_silu(c_ref[...]), w_ref[...]) + b_ref[...]


def _ada(cond, ada_w, ada_b):
    n_layers = ada_w.shape[0]
    tn = 1024
    out = pl.pallas_call(
        _ada_body,
        grid=(n_layers, 6 * D // tn),
        in_specs=[
            pl.BlockSpec((N_COND, D), lambda l, j: (0, 0)),
            pl.BlockSpec((None, D, tn), lambda l, j: (l, 0, j)),
            pl.BlockSpec((None, 1, tn), lambda l, j: (l, 0, j)),
        ],
        out_specs=pl.BlockSpec((None, N_COND, tn), lambda l, j: (l, 0, j)),
        out_shape=jax.ShapeDtypeStruct((n_layers, N_COND, 6 * D), F32),
        compiler_params=_params(("arbitrary", "arbitrary")),
        name="ada",
    )(cond, ada_w, ada_b.reshape(n_layers, 1, 6 * D))
    return out.reshape(n_layers, N_COND, 6, D)


def _prenorm(x, nw, sc, sh):
    ms = jnp.mean(x * x, axis=-1, keepdims=True)
    return (x * lax.rsqrt(ms + EPS) * nw) * (1.0 + sc) + sh


def _short_conv(v, w_ref, row_len):
    n = v.shape[0]
    pos = lax.broadcasted_iota(jnp.int32, (n, 1), 0) & (row_len - 1)
    prev = jnp.where(pos == 0, 0.0, pltpu.roll(v, 1, 0))
    nxt = jnp.where(pos == row_len - 1, 0.0, pltpu.roll(v, n - 1, 0))
    return prev * w_ref[0:1, :] + v * w_ref[1:2, :] + nxt * w_ref[2:3, :]


RT = 1024
TILES = 2


def _route_body(h_ref, rwt_ref, rb_ref, idx_ref, lrank_ref, gate_ref, cnt_ref):
    logits = lax.dot_general(rwt_ref[...], h_ref[...], (((1,), (1,)), ((), ())), preferred_element_type=F32,
                             precision=HIGHEST) + rb_ref[...]
    eid = lax.broadcasted_iota(jnp.int32, (N_EXPERTS, RT), 0)
    work = logits
    sel = jnp.zeros((N_EXPERTS, RT), F32)
    vals, ids, hots = [], [], []
    for _ in range(TOP_K):
        m = jnp.max(work, axis=0, keepdims=True)
        ik = jnp.min(jnp.where(work == m, eid, N_EXPERTS), axis=0, keepdims=True)
        hot = eid == ik
        work = jnp.where(hot, NEG, work)
        sel = sel + hot.astype(F32)
        vals.append(m)
        ids.append(ik)
        hots.append(hot)
    es = [jnp.exp(v - vals[0]) for v in vals]
    den = es[0] + es[1] + es[2] + es[3]
    earlier = (lax.broadcasted_iota(jnp.int32, (TM, TM), 0) < lax.broadcasted_iota(jnp.int32, (TM, TM), 1))
    earlier = earlier.astype(BF16)
    ranks, cnts = [], []
    for j in range(RT // TM):
        part = sel[:, j * TM:(j + 1) * TM]
        ranks.append(jnp.dot(part.astype(BF16), earlier, preferred_element_type=F32))
        cnts.append(jnp.broadcast_to(jnp.sum(part, axis=1, keepdims=True), (N_EXPERTS, TM)))
    rank_all = jnp.concatenate(ranks, axis=1)
    cnt_ref[...] = jnp.concatenate(cnts, axis=1).astype(jnp.int32)
    slot = lax.broadcasted_iota(jnp.int32, (8, RT), 0)
    idx_o = jnp.zeros((8, RT), jnp.int32)
    rank_o = jnp.zeros((8, RT), jnp.int32)
    gate_o = jnp.zeros((8, RT), F32)
    for k in range(TOP_K):
        rk = jnp.sum(jnp.where(hots[k], rank_all, 0.0), axis=0, keepdims=True)
        idx_o = jnp.where(slot == k, ids[k], idx_o)
        rank_o = jnp.where(slot == k, rk.astype(jnp.int32), rank_o)
        gate_o = jnp.where(slot == k, es[k] / den, gate_o)
    idx_ref[...] = idx_o
    lrank_ref[...] = rank_o
    gate_ref[...] = gate_o


def _route(h2, rwt, rb):
    n_tok = h2.shape[0]
    const = lambda i: (0, 0)
    cols = lambda i: (0, i)
    return pl.pallas_call(
        _route_body,
        grid=(n_tok // RT,),
        in_specs=[
            pl.BlockSpec((RT, D), lambda i: (i, 0)),
            pl.BlockSpec((N_EXPERTS, D), const),
            pl.BlockSpec((N_EXPERTS, 1), const),
        ],
        out_specs=[
            pl.BlockSpec((8, RT), cols),
            pl.BlockSpec((8, RT), cols),
            pl.BlockSpec((8, RT), cols),
            pl.BlockSpec((N_EXPERTS, RT), cols),
        ],
        out_shape=[
            jax.ShapeDtypeStruct((8, n_tok), jnp.int32),
            jax.ShapeDtypeStruct((8, n_tok), jnp.int32),
            jax.ShapeDtypeStruct((8, n_tok), F32),
            jax.ShapeDtypeStruct((N_EXPERTS, n_tok), jnp.int32),
        ],
        compiler_params=_params(("arbitrary",)),
        name="moe_route",
    )(h2, rwt, rb)


def _tile_maps(n_ctx_tiles, tiles_per_seq):
    def grp(i):
        return jnp.where(i < n_ctx_tiles, 0, 1 + (i - n_ctx_tiles) // tiles_per_seq)

    tok = lambda i: (i, 0)
    const = lambda i: (0, 0)
    mod = lambda i: (grp(i), 0, 0)
    return tok, const, mod


def _mixer_outs(n_tok, tok):
    return ([pl.BlockSpec((TILES * TM, D), tok), pl.BlockSpec((TILES * TM, D), tok)],
            [jax.ShapeDtypeStruct((n_tok, D), F32), jax.ShapeDtypeStruct((n_tok, D), F32)])


def _moe_input(x1, mod_ref, n2_ref, x1_ref, h2_ref):
    x1_ref[...] = x1
    h2_ref[...] = _prenorm(x1, n2_ref[...], mod_ref[4:5, :], mod_ref[3:4, :])


def _interleave(gens):
    live = list(gens)
    while live:
        live = [g for g in live if next(g, StopIteration) is not StopIteration]


def _tile_views(refs, g):
    return [r.at[pl.ds(g * TM, TM)] for r in refs]


def _conv_tile(from_ctx, row_len, xp_ref, xs_ref, mod_ref, n1_ref, win_ref, cw_ref, wout_ref, n2_ref, x1_ref, h2_ref):
    x = jnp.where(from_ctx, xp_ref[...], xs_ref[...])
    h = _prenorm(x, n1_ref[...], mod_ref[1:2, :], mod_ref[0:1, :]).astype(BF16)
    yield
    proj = jnp.dot(h, win_ref[...], preferred_element_type=F32)
    yield
    b, cg, u = proj[:, :D], proj[:, D:2 * D], proj[:, 2 * D:]
    gated = (b * _short_conv(cg * u, cw_ref, row_len)).astype(BF16)
    yield
    mix = jnp.dot(gated, wout_ref[...], preferred_element_type=F32)
    yield
    _moe_input(x + mod_ref[2:3, :] * mix, mod_ref, n2_ref, x1_ref, h2_ref)
    yield


def _conv_body(n_ctx_steps, xp_ref, xs_ref, mod_ref, n1_ref, win_ref, cw_ref, wout_ref, n2_ref, x1_ref, h2_ref):
    i = pl.program_id(0)
    from_ctx = i < n_ctx_steps
    row_len = jnp.where(from_ctx, CTX_ROW, GRID_W)
    tiles = []
    for g in range(TILES):
        xp, xs, x1, h2 = _tile_views([xp_ref, xs_ref, x1_ref, h2_ref], g)
        tiles.append(_conv_tile(from_ctx, row_len, xp, xs, mod_ref, n1_ref, win_ref, cw_ref, wout_ref, n2_ref, x1, h2))
    _interleave(tiles)


def _mixer_grid(n_tok, n_ctx_tiles, tiles_per_seq):
    assert n_ctx_tiles % TILES == 0 and tiles_per_seq % TILES == 0
    return n_tok // (TILES * TM), n_ctx_tiles // TILES, tiles_per_seq // TILES


def _conv_layer(x_ctx, x_dec, mod, n1, win, cw, wout, n2, n_ctx_tiles, tiles_per_seq):
    n_tok = x_ctx.shape[0] + x_dec.shape[0]
    n_steps, n_ctx_steps, steps_per_seq = _mixer_grid(n_tok, n_ctx_tiles, tiles_per_seq)
    tok, const, modm = _tile_maps(n_ctx_steps, steps_per_seq)
    out_specs, out_shapes = _mixer_outs(n_tok, tok)
    return pl.pallas_call(
        functools.partial(_conv_body, n_ctx_steps),
        grid=(n_steps,),
        in_specs=[
            pl.BlockSpec((TILES * TM, D), lambda i: (jnp.minimum(i, n_ctx_steps - 1), 0)),
            pl.BlockSpec((TILES * TM, D), lambda i: (jnp.maximum(i - n_ctx_steps, 0), 0)),
            pl.BlockSpec((None, 6, D), modm),
            pl.BlockSpec((1, D), const),
            pl.BlockSpec((D, 3 * D), const),
            pl.BlockSpec((3, D), const),
            pl.BlockSpec((D, D), const),
            pl.BlockSpec((1, D), const),
        ],
        out_specs=out_specs,
        out_shape=out_shapes,
        compiler_params=_params(("arbitrary",)),
        name="conv_layer",
    )(x_ctx, x_dec, mod, n1, win, cw, wout, n2)


RUN_ALIGN = 8
RUN_SIZES = (256, 128, 64, 32, 16, 8)
SORT_ROWS = TM * TOP_K + N_EXPERTS * RUN_ALIGN


def _run_pieces(src_ref, dst_ref, len_ref, tile):
    for e in range(N_EXPERTS):
        n = len_ref[tile * N_EXPERTS + e]
        src = src_ref[tile * N_EXPERTS + e]
        dst = dst_ref[tile * N_EXPERTS + e]
        for size in RUN_SIZES:
            done = n & ~(2 * size - 1)
            yield ((n & size) != 0, pl.multiple_of(src + done, RUN_ALIGN), pl.multiple_of(dst + done, RUN_ALIGN),
                   size)


def _sorted_onehot(lpos, rows_major):
    if rows_major:
        j = lax.broadcasted_iota(jnp.int32, (SORT_ROWS, TM), 0)
        return [j == lpos[k:k + 1, :] for k in range(TOP_K)]
    j = lax.broadcasted_iota(jnp.int32, (TM, SORT_ROWS), 1)
    return [j == lpos[:, k:k + 1] for k in range(TOP_K)]


def _scatter_body(src_ref, dst_ref, len_ref, pend_ref, lpos_ref, h_ref, xs_ref, sorted_ref, zbuf, sem, zsem):
    i = pl.program_id(0)

    @pl.when(i == 0)
    def _():
        zbuf[...] = jnp.zeros_like(zbuf)

        def zero_block(j):
            return pltpu.make_async_copy(zbuf, xs_ref.at[pl.ds(pl.multiple_of(j * BLK, BLK), BLK)], zsem)

        def start(j, carry):
            zero_block(j).start()
            return carry

        def wait(j, carry):
            zero_block(j).wait()
            return carry

        n_used = pend_ref[N_EXPERTS - 1] // BLK
        lax.fori_loop(n_used, xs_ref.shape[0] // BLK, start, 0)
        lax.fori_loop(n_used, xs_ref.shape[0] // BLK, wait, 0)
        for phase in range(2):
            for e in range(N_EXPERTS):
                end = pend_ref[e]
                begin = pend_ref[e - 1] if e else 0

                @pl.when(end > begin)
                def _():
                    tail = pl.ds(pl.multiple_of(end - BLK, BLK), BLK)
                    cp = pltpu.make_async_copy(zbuf, xs_ref.at[tail], zsem)
                    if phase == 0:
                        cp.start()
                    else:
                        cp.wait()

    def runs(tile, slot, start):
        for cond, src, dst, size in _run_pieces(src_ref, dst_ref, len_ref, tile):
            @pl.when(cond)
            def _():
                cp = pltpu.make_async_copy(sorted_ref.at[slot, pl.ds(src, size)], xs_ref.at[pl.ds(dst, size)],
                                           sem.at[slot])
                if start:
                    cp.start()
                else:
                    cp.wait()

    slot = i % 2
    hots = _sorted_onehot(lpos_ref[...], True)
    onehot = (hots[0] | hots[1] | hots[2] | hots[3]).astype(BF16)
    sorted_ref[slot] = _pack_bf16_exact(jnp.dot(onehot, h_ref[...].astype(BF16), preferred_element_type=F32))
    runs(i, slot, True)

    @pl.when(i > 0)
    def _():
        runs(i - 1, 1 - slot, False)

    @pl.when(i == pl.num_programs(0) - 1)
    def _():
        runs(i, slot, False)


def _scatter(src_start, dst_start, run_len, pend, lpos, h2, n_slots):
    n_tok = h2.shape[0]
    return pl.pallas_call(
        _scatter_body,
        grid_spec=pltpu.PrefetchScalarGridSpec(
            num_scalar_prefetch=4,
            grid=(n_tok // TM,),
            in_specs=[pl.BlockSpec((8, TM), lambda i, *_: (0, i)), pl.BlockSpec((TM, D), lambda i, *_: (i, 0))],
            out_specs=pl.BlockSpec(memory_space=pl.ANY),
            scratch_shapes=[pltpu.VMEM((2, SORT_ROWS, DP), jnp.uint32), pltpu.VMEM((BLK, DP), jnp.uint32),
                            pltpu.SemaphoreType.DMA((2,)), pltpu.SemaphoreType.DMA],
        ),
        out_shape=jax.ShapeDtypeStruct((n_slots, DP), jnp.uint32),
        compiler_params=_params(("arbitrary",)),
        name="moe_scatter",
    )(src_start, dst_start, run_len, pend, lpos, h2)


W_ROWS = 128


def _expert_body(be_ref, nu_ref, x_ref, wgu_ref, bgu_ref, wd_ref, bd_ref, y_ref, wgu_bf, wd_bf):
    j = pl.program_id(0)
    n_used = nu_ref[0]
    jj = jnp.minimum(j, n_used - 1)
    e = be_ref[jj]
    e_prev = be_ref[jnp.maximum(jj - 1, 0)]

    @pl.when((j == 0) | (e != e_prev))
    def _():
        def cast(r, carry):
            rows = pl.ds(pl.multiple_of(r * W_ROWS, W_ROWS), W_ROWS)
            wgu_bf[rows, :] = wgu_ref[rows, :].astype(BF16)
            wd_bf[rows, :] = wd_ref[rows, :].astype(BF16)
            return carry

        lax.fori_loop(0, D // W_ROWS, cast, 0)

    @pl.when(j < n_used)
    def _():
        hh = jnp.dot(_unpack_bf16(x_ref[...]), wgu_bf[...], preferred_element_type=F32) + bgu_ref[...]
        hg = jnp.minimum(hh[:, :D_FF], SWIGLU_LIMIT)
        hl = jnp.clip(hh[:, D_FF:], -SWIGLU_LIMIT, SWIGLU_LIMIT)
        act = hg * jax.nn.sigmoid(SWIGLU_ALPHA * hg) * (hl + 1.0)
        y_ref[...] = _pack_bf16(jnp.dot(act.astype(BF16), wd_bf[...], preferred_element_type=F32) + bd_ref[...])

    @pl.when(j >= n_used)
    def _():
        y_ref[...] = jnp.zeros_like(y_ref)


def _experts(block_e, n_used, xs, layer, wgu, bgu, wd, bd):
    n_slots = xs.shape[0]
    n_layers = wgu.shape[0]

    def blk(j, be, nu):
        return (jnp.minimum(j, nu[0] - 1), 0)

    def wmap(j, be, nu):
        return (layer, be[jnp.minimum(j, nu[0] - 1)], 0, 0)

    return pl.pallas_call(
        _expert_body,
        grid_spec=pltpu.PrefetchScalarGridSpec(
            num_scalar_prefetch=2,
            grid=(n_slots // BLK,),
            in_specs=[
                pl.BlockSpec((BLK, DP), blk),
                pl.BlockSpec((None, None, D, 2 * D_FF), wmap),
                pl.BlockSpec((None, None, 1, 2 * D_FF), wmap),
                pl.BlockSpec((None, None, D_FF, D), wmap),
                pl.BlockSpec((None, None, 1, D), wmap),
            ],
            out_specs=pl.BlockSpec((BLK, DP), lambda j, be, nu: (j, 0)),
            scratch_shapes=[pltpu.VMEM((D, 2 * D_FF), BF16), pltpu.VMEM((D_FF, D), BF16)],
        ),
        out_shape=jax.ShapeDtypeStruct((n_slots, DP), jnp.uint32),
        compiler_params=_params(("arbitrary",)),
        name="moe_experts",
    )(block_e, n_used, xs, wgu, bgu.reshape(n_layers, N_EXPERTS, 1, 2 * D_FF), wd,
      bd.reshape(n_layers, N_EXPERTS, 1, D))


def _combine_body(n_ctx_tiles, src_ref, dst_ref, len_ref, x_ref, lpos_ref, gate_ref, mod_ref, fnw_ref, y_ref, *rest):
    *o_refs, buf, sem = rest
    i = pl.program_id(0)
    n_tiles = pl.num_programs(0)

    def runs(tile, slot, start):
        for cond, src, dst, size in _run_pieces(src_ref, dst_ref, len_ref, tile):
            @pl.when(cond)
            def _():
                cp = pltpu.make_async_copy(y_ref.at[pl.ds(dst, size)], buf.at[slot, pl.ds(src, size)], sem.at[slot])
                if start:
                    cp.start()
                else:
                    cp.wait()

    @pl.when(i == 0)
    def _():
        buf[...] = jnp.zeros_like(buf)
        runs(0, 0, True)

    @pl.when(i + 1 < n_tiles)
    def _():
        runs(i + 1, (i + 1) % 2, True)

    slot = i % 2
    runs(i, slot, False)
    gate = gate_ref[...].T
    hots = _sorted_onehot(lpos_ref[...].T, False)
    placed = jnp.where(hots[0], gate[:, 0:1], 0.0)
    for k in range(1, TOP_K):
        placed = jnp.where(hots[k], gate[:, k:k + 1], placed)
    acc = jnp.dot(placed.astype(BF16), _unpack_bf16(buf[slot]), preferred_element_type=F32)
    x2 = x_ref[...] + mod_ref[5:6, :] * acc
    if n_ctx_tiles is None:
        o_refs[0][...] = x2
    else:
        ms = jnp.mean(x2 * x2, axis=-1, keepdims=True)
        y = x2 * lax.rsqrt(ms + EPS) * fnw_ref[...]

        @pl.when(i < n_ctx_tiles)
        def _():
            o_refs[0][...] = y

        @pl.when(i >= n_ctx_tiles)
        def _():
            o_refs[1][...] = y


def _combine(src_start, dst_start, run_len, x1, lpos, gate, mod, fnw, y_slots, n_ctx_tiles, tiles_per_seq, final):
    n_tok = x1.shape[0]

    def grp(i):
        return jnp.where(i < n_ctx_tiles, 0, 1 + (i - n_ctx_tiles) // tiles_per_seq)

    if final:
        out_specs = [pl.BlockSpec((TM, D), lambda i, *_: (jnp.minimum(i, n_ctx_tiles - 1), 0)),
                     pl.BlockSpec((TM, D), lambda i, *_: (jnp.maximum(i - n_ctx_tiles, 0), 0))]
        out_shape = [jax.ShapeDtypeStruct((n_ctx_tiles * TM, D), F32),
                     jax.ShapeDtypeStruct((n_tok - n_ctx_tiles * TM, D), F32)]
    else:
        out_specs = [pl.BlockSpec((TM, D), lambda i, *_: (i, 0))]
        out_shape = [jax.ShapeDtypeStruct((n_tok, D), F32)]
    return pl.pallas_call(
        functools.partial(_combine_body, n_ctx_tiles if final else None),
        grid_spec=pltpu.PrefetchScalarGridSpec(
            num_scalar_prefetch=3,
            grid=(n_tok // TM,),
            in_specs=[
                pl.BlockSpec((TM, D), lambda i, *_: (i, 0)),
                pl.BlockSpec((8, TM), lambda i, *_: (0, i)),
                pl.BlockSpec((8, TM), lambda i, *_: (0, i)),
                pl.BlockSpec((None, 6, D), lambda i, *_: (grp(i), 0, 0)),
                pl.BlockSpec((1, D), lambda i, *_: (0, 0)),
                pl.BlockSpec(memory_space=pl.ANY),
            ],
            out_specs=out_specs,
            scratch_shapes=[pltpu.VMEM((2, SORT_ROWS, DP), jnp.uint32), pltpu.SemaphoreType.DMA((2,))],
        ),
        out_shape=out_shape,
        compiler_params=_params(("arbitrary",)),
        name="moe_combine",
    )(src_start, dst_start, run_len, x1, lpos, gate, mod, fnw, y_slots)


def _moe(x1, h2, rwt, rb, mod, fnw, layer, wgu, bgu, wd, bd, n_ctx_tiles, tiles_per_seq, final):
    n_tok = x1.shape[0]
    n_tiles = n_tok // TM
    idx, lrank, gate, cnt_tok = _route(h2, rwt, rb)
    cnt = cnt_tok[:, ::TM].T
    run_len = (cnt + RUN_ALIGN - 1) // RUN_ALIGN * RUN_ALIGN
    src_start = jnp.cumsum(run_len, axis=1) - run_len
    total = jnp.sum(run_len, axis=0)
    padded = (total + BLK - 1) // BLK * BLK
    pend = jnp.cumsum(padded)
    dst_start = (pend - padded)[None, :] + jnp.cumsum(run_len, axis=0) - run_len
    hit = idx[:TOP_K, None, :] == jnp.arange(N_EXPERTS, dtype=jnp.int32)[None, :, None]
    src_tok = jnp.repeat(src_start.T, TM, axis=1)
    lpos = jnp.sum(jnp.where(hit, src_tok[None], 0), axis=1) + lrank[:TOP_K]
    lpos = jnp.concatenate([lpos, jnp.full((8 - TOP_K, n_tok), -1, jnp.int32)], axis=0).astype(jnp.int32)
    n_slots = (n_tok * TOP_K + n_tiles * N_EXPERTS * (RUN_ALIGN - 1)) // BLK * BLK + (N_EXPERTS + 1) * BLK
    n_blocks = n_slots // BLK
    block_start = jnp.arange(n_blocks, dtype=jnp.int32)[:, None] * BLK
    block_e = jnp.minimum(jnp.sum(pend[None, :] <= block_start, axis=1), N_EXPERTS - 1).astype(jnp.int32)
    n_used = (pend[-1:] // BLK).astype(jnp.int32)
    tables = [t.reshape(-1).astype(jnp.int32) for t in (src_start, dst_start, run_len)]
    xs = _scatter(*tables, pend.astype(jnp.int32), lpos, h2, n_slots)
    ys = _experts(block_e, n_used, xs, layer, wgu, bgu, wd, bd)
    return _combine(*tables, x1, lpos, gate, mod, fnw, ys, n_ctx_tiles, tiles_per_seq, final)


def _dn_in_tile(row_len, x_ref, mod_ref, n1_ref, w_ref, wba_ref, cw_ref, alog_ref, dtb_ref,
                q_ref, k_ref, v_ref, z_ref, bg_ref):
    h = _prenorm(x_ref[...], n1_ref[...], mod_ref[1:2, :], mod_ref[0:1, :]).astype(BF16)
    yield
    proj = jnp.dot(h, w_ref[...], preferred_element_type=F32)
    yield
    z_ref[...] = proj[:, CONV_CH:].astype(BF16)
    qkv = _silu(_short_conv(proj[:, :CONV_CH], cw_ref, row_len))
    yield
    for hd in range(2 * HK):
        s = qkv[:, hd * DK:(hd + 1) * DK]
        n = s * lax.rsqrt(jnp.sum(s * s, axis=-1, keepdims=True) + EPS)
        if hd < HK:
            q_ref[:, hd * DK:(hd + 1) * DK] = n * (DK ** -0.5)
        else:
            k_ref[:, (hd - HK) * DK:(hd - HK + 1) * DK] = n
    v_ref[...] = qkv[:, 2 * QK_DIM:]
    yield
    ba = jnp.dot(h, wba_ref[...], preferred_element_type=F32)
    beta = jax.nn.sigmoid(ba)
    a = ba + dtb_ref[...]
    softplus = jnp.maximum(a, 0.0) + jnp.log(1.0 + jnp.exp(-jnp.abs(a)))
    g = -jnp.exp(alog_ref[...]) * softplus
    lane = lax.broadcasted_iota(jnp.int32, (TM, LANES), 1)
    bg_ref[...] = jnp.where(lane < G0, beta, g)
    yield


def _dn_in_body(n_ctx_steps, x_ref, mod_ref, n1_ref, w_ref, wba_ref, cw_ref, alog_ref, dtb_ref,
                q_ref, k_ref, v_ref, z_ref, bg_ref):
    row_len = jnp.where(pl.program_id(0) < n_ctx_steps, CTX_ROW, GRID_W)
    tiles = []
    for g in range(TILES):
        x, q, k, v, z, bg = _tile_views([x_ref, q_ref, k_ref, v_ref, z_ref, bg_ref], g)
        tiles.append(_dn_in_tile(row_len, x, mod_ref, n1_ref, w_ref, wba_ref, cw_ref, alog_ref, dtb_ref, q, k, v, z, bg))
    _interleave(tiles)


def _dn_in(x, mod, n1, w, wba, cw, alog, dtb, n_ctx_tiles, tiles_per_seq):
    n_tok = x.shape[0]
    n_steps, n_ctx_steps, steps_per_seq = _mixer_grid(n_tok, n_ctx_tiles, tiles_per_seq)
    tok, const, modm = _tile_maps(n_ctx_steps, steps_per_seq)
    rows = TILES * TM
    return pl.pallas_call(
        functools.partial(_dn_in_body, n_ctx_steps),
        grid=(n_steps,),
        in_specs=[
            pl.BlockSpec((rows, D), tok),
            pl.BlockSpec((None, 6, D), modm),
            pl.BlockSpec((1, D), const),
            pl.BlockSpec((D, CONV_CH + V_DIM), const),
            pl.BlockSpec((D, LANES), const),
            pl.BlockSpec((3, CONV_CH), const),
            pl.BlockSpec((1, LANES), const),
            pl.BlockSpec((1, LANES), const),
        ],
        out_specs=[
            pl.BlockSpec((rows, QK_DIM), tok),
            pl.BlockSpec((rows, QK_DIM), tok),
            pl.BlockSpec((rows, V_DIM), tok),
            pl.BlockSpec((rows, V_DIM), tok),
            pl.BlockSpec((rows, LANES), tok),
        ],
        out_shape=[
            jax.ShapeDtypeStruct((n_tok, QK_DIM), F32),
            jax.ShapeDtypeStruct((n_tok, QK_DIM), F32),
            jax.ShapeDtypeStruct((n_tok, V_DIM), F32),
            jax.ShapeDtypeStruct((n_tok, V_DIM), BF16),
            jax.ShapeDtypeStruct((n_tok, LANES), F32),
        ],
        compiler_params=_params(("arbitrary",)),
        name="dn_in",
    )(x, mod, n1, w, wba, cw, alog, dtb)


def _hdot(a, b):
    return jnp.dot(a, b, preferred_element_type=F32, precision=HIGHEST)


def _nt_dot(a, b):
    return lax.dot_general(a, b, (((1,), (1,)), ((), ())), preferred_element_type=F32)


def _delta_dir(bwd, q_ref, k_ref, v_ref, bg_ref, o_ref, state):
    r2 = lax.broadcasted_iota(jnp.int32, (CHUNK, LANES), 0)
    lane = lax.broadcasted_iota(jnp.int32, (CHUNK, LANES), 1)
    c2 = lane & (CHUNK - 1)
    left = lane < CHUNK
    causal2 = (r2 <= c2) if bwd else (r2 >= c2)
    strict2 = (r2 < c2) if bwd else (r2 > c2)
    eye2 = (r2 == c2).astype(F32)
    lvl = r2 ^ c2
    b0 = HV if bwd else 0
    g0 = 2 * HV + b0
    bg = bg_ref[...]
    gc = _hdot(causal2[:, :CHUNK].astype(F32), bg)
    gct = gc.T
    gtot = jnp.sum(bg, axis=0, keepdims=True)
    egc = jnp.exp(gc)
    eend = jnp.exp(gtot - gc)
    etot = jnp.exp(gtot)

    def pair_cols(a, j):
        return jnp.where(left, a[:, j:j + 1], a[:, j + 1:j + 2])

    def bdiag(x):
        z = jnp.zeros_like(x)
        return jnp.concatenate([jnp.where(left, x, z), jnp.where(left, z, x)], axis=0).astype(BF16)

    eye_k = (lax.broadcasted_iota(jnp.int32, (DK, DK), 0) == lax.broadcasted_iota(jnp.int32, (DK, DK), 1)).astype(BF16)
    ns, attns, kts = [], [], []
    for p in range(HK):
        j = g0 + PAIR * p
        k16 = k_ref[:, p * DK:(p + 1) * DK].astype(BF16)
        q16 = q_ref[:, p * DK:(p + 1) * DK].astype(BF16)
        kq = _nt_dot(jnp.concatenate([k16, q16], axis=0), jnp.concatenate([k16, k16], axis=0))
        grow2 = jnp.concatenate([gct[j:j + 1, :], gct[j + 1:j + 2, :]], axis=1)
        decay2 = jnp.exp(jnp.where(causal2, pair_cols(gc, j) - grow2, NEG))
        ns.append(jnp.where(strict2, kq[:CHUNK] * decay2 * pair_cols(bg, b0 + PAIR * p), 0.0))
        attns.append(kq[CHUNK:] * decay2)
        kts.append(_nt_dot(eye_k, k16).astype(BF16))
        yield

    def mdot(a, b):
        return jnp.dot(a.astype(BF16), bdiag(b), preferred_element_type=F32)

    n4 = [jnp.where(lvl < 4, n, 0.0) for n in ns]
    sq = [mdot(a, a) for a in n4]
    yield
    ts = [eye2 - a for a in n4]
    ts = [t + mdot(t, s) for t, s in zip(ts, sq)]
    yield
    for bit in range(2, 6):
        ys = [mdot(t, jnp.where((lvl >> bit) == 1, n, 0.0)) for t, n in zip(ts, ns)]
        yield
        ts = [t - mdot(y, t) for t, y in zip(ts, ys)]
        yield

    sols = []
    for p in range(HK):
        kh = k_ref[:, p * DK:(p + 1) * DK]
        rhs = []
        for e in range(PAIR):
            h = PAIR * p + e
            beta = bg[:, b0 + h:b0 + h + 1]
            rhs.append(jnp.concatenate(
                [v_ref[:, h * DV:(h + 1) * DV] * beta, kh * (beta * egc[:, g0 + h:g0 + h + 1])], axis=1))
        sols.append(jnp.dot(bdiag(ts[p]), jnp.concatenate(rhs, axis=0).astype(BF16), preferred_element_type=F32))
        yield

    outs = []
    for h in range(HV):
        p, e = divmod(h, PAIR)
        qg = q_ref[:, p * DK:(p + 1) * DK] * egc[:, g0 + h:g0 + h + 1]
        lhs = jnp.concatenate([sols[p][e * CHUNK:(e + 1) * CHUNK, DV:], qg], axis=0)
        outs.append(_bdot(lhs, state[bwd, h]))
        yield

    for p in range(HK):
        h0 = PAIR * p
        vns = [sols[p][e * CHUNK:(e + 1) * CHUNK, :DV] - outs[h0 + e][:CHUNK] for e in range(PAIR)]
        intra = jnp.dot(bdiag(attns[p]), jnp.concatenate(vns, axis=0).astype(BF16), preferred_element_type=F32)
        for e in range(PAIR):
            o_ref[:, (h0 + e) * DV:(h0 + e + 1) * DV] = (outs[h0 + e][CHUNK:] + intra[e * CHUNK:(e + 1) * CHUNK]).astype(BF16)
        scaled = jnp.concatenate([vns[e] * eend[:, g0 + h0 + e:g0 + h0 + e + 1] for e in range(PAIR)], axis=1)
        kv = jnp.dot(kts[p], scaled.astype(BF16), preferred_element_type=F32)
        for e in range(PAIR):
            h = h0 + e
            state[bwd, h] = state[bwd, h] * etot[:, g0 + h:g0 + h + 1] + kv[:, e * DV:(e + 1) * DV]
        yield


FIRST, LAST, DEC = 1, 2, 4


def _delta_body(fb_ref, bb_ref, si_ref, oi_ref, fl_ref, qf_ref, kf_ref, vf_ref, bgf_ref, qb_ref, kb_ref, vb_ref,
                bgb_ref, s0_ref, of_ref, ob_ref, sout_ref, state):
    flags = fl_ref[pl.program_id(0)]

    @pl.when((flags & FIRST) != 0)
    def _():
        state[...] = jnp.where((flags & DEC) != 0, s0_ref[...], 0.0)

    _interleave([_delta_dir(0, qf_ref, kf_ref, vf_ref, bgf_ref, of_ref, state),
                 _delta_dir(1, qb_ref, kb_ref, vb_ref, bgb_ref, ob_ref, state)])

    @pl.when((flags & (LAST | DEC)) == LAST)
    def _():
        sout_ref[...] = state[...]


def _delta(q, k, v, bg, state_delta, n_ctx, ctx_chunks, n_dec, dec_chunks):
    n_tok = q.shape[0]
    fb, bb, si, oi, fl = [], [], [], [], []
    for dec, n_seq, n, base in ((0, n_ctx, ctx_chunks, 0), (1, n_dec, dec_chunks, n_ctx * ctx_chunks)):
        for b in range(n_seq):
            for c in range(n):
                fb.append(base + b * n + c)
                bb.append(base + b * n + n - 1 - c)
                si.append(b if dec else 0)
                oi.append(n_ctx - 1 if dec else b)
                fl.append((FIRST if c == 0 else 0) | (LAST if c == n - 1 else 0) | (DEC if dec else 0))
    tables = [jnp.asarray(t, jnp.int32) for t in (fb, bb, si, oi, fl)]
    fwd = lambda s, fb, bb, si, oi, fl: (fb[s], 0)
    bwd = lambda s, fb, bb, si, oi, fl: (bb[s], 0)
    tok_specs = lambda m: [pl.BlockSpec((CHUNK, QK_DIM), m), pl.BlockSpec((CHUNK, QK_DIM), m),
                           pl.BlockSpec((CHUNK, V_DIM), m), pl.BlockSpec((CHUNK, LANES), m)]
    return pl.pallas_call(
        _delta_body,
        grid_spec=pltpu.PrefetchScalarGridSpec(
            num_scalar_prefetch=5,
            grid=(len(fb),),
            in_specs=tok_specs(fwd) + tok_specs(bwd) + [
                pl.BlockSpec((None, None, 2, HV, DK, DV), lambda s, fb, bb, si, oi, fl: (si[s], 0, 0, 0, 0, 0)),
            ],
            out_specs=[
                pl.BlockSpec((CHUNK, V_DIM), fwd),
                pl.BlockSpec((CHUNK, V_DIM), bwd),
                pl.BlockSpec((None, 2, HV, DK, DV), lambda s, fb, bb, si, oi, fl: (oi[s], 0, 0, 0, 0)),
            ],
            scratch_shapes=[pltpu.VMEM((2, HV, DK, DV), F32)],
        ),
        out_shape=[
            jax.ShapeDtypeStruct((n_tok, V_DIM), BF16),
            jax.ShapeDtypeStruct((n_tok, V_DIM), BF16),
            jax.ShapeDtypeStruct((n_ctx, 2, HV, DK, DV), F32),
        ],
        compiler_params=_params(("arbitrary",)),
        name="delta_rule",
    )(*tables, q, k, v, bg, q, k, v, bg, state_delta)


def _dn_out_tile(of_ref, ob_ref, z_ref, x_ref, mod_ref, nw_ref, wout_ref, n2_ref, x1_ref, h2_ref):
    o = of_ref[...].astype(F32) + ob_ref[...].astype(F32)
    z = z_ref[...].astype(F32)
    parts = []
    for h in range(HV):
        oh = o[:, h * DV:(h + 1) * DV]
        nh = oh * lax.rsqrt(jnp.mean(oh * oh, axis=-1, keepdims=True) + EPS) * nw_ref[...]
        parts.append((nh * _silu(z[:, h * DV:(h + 1) * DV])).astype(BF16))
    gated = jnp.concatenate(parts, axis=1)
    yield
    mix = jnp.dot(gated, wout_ref[...], preferred_element_type=F32)
    yield
    _moe_input(x_ref[...] + mod_ref[2:3, :] * mix, mod_ref, n2_ref, x1_ref, h2_ref)
    yield


def _dn_out_body(of_ref, ob_ref, z_ref, x_ref, mod_ref, nw_ref, wout_ref, n2_ref, x1_ref, h2_ref):
    tiles = []
    for g in range(TILES):
        of, ob, z, x, x1, h2 = _tile_views([of_ref, ob_ref, z_ref, x_ref, x1_ref, h2_ref], g)
        tiles.append(_dn_out_tile(of, ob, z, x, mod_ref, nw_ref, wout_ref, n2_ref, x1, h2))
    _interleave(tiles)


def _dn_out(o_f, o_b, z, x, mod, nw, wout, n2, n_ctx_tiles, tiles_per_seq):
    n_tok = x.shape[0]
    n_steps, n_ctx_steps, steps_per_seq = _mixer_grid(n_tok, n_ctx_tiles, tiles_per_seq)
    tok, const, modm = _tile_maps(n_ctx_steps, steps_per_seq)
    out_specs, out_shapes = _mixer_outs(n_tok, tok)
    rows = TILES * TM
    return pl.pallas_call(
        _dn_out_body,
        grid=(n_steps,),
        in_specs=[
            pl.BlockSpec((rows, V_DIM), tok),
            pl.BlockSpec((rows, V_DIM), tok),
            pl.BlockSpec((rows, V_DIM), tok),
            pl.BlockSpec((rows, D), tok),
            pl.BlockSpec((None, 6, D), modm),
            pl.BlockSpec((1, DV), const),
            pl.BlockSpec((V_DIM, D), const),
            pl.BlockSpec((1, D), const),
        ],
        out_specs=out_specs,
        out_shape=out_shapes,
        compiler_params=_params(("arbitrary",)),
        name="dn_out",
    )(o_f, o_b, z, x, mod, nw, wout, n2)


def kernel(x_prompt, x_sample, state_delta, c, c_ctx, ada_w, ada_b, norm1_w, norm2_w, conv_in_w, conv_w,
           conv_out_w, dn_in_w, dn_conv_w, dn_a_log, dn_dt_bias, dn_norm_w, dn_out_w, router_w, router_b,
           exp_gu_w, exp_gu_b, exp_down_w, exp_down_b, final_norm_w):
    n_ctx, ctx_len, _ = x_prompt.shape
    n_dec, dec_len, _ = x_sample.shape
    assert ctx_len == CTX_ROW and dec_len % TM == 0 and n_dec + 1 <= N_COND
    n_ctx_tok = n_ctx * ctx_len
    n_ctx_tiles = n_ctx_tok // TM
    tiles_per_seq = dec_len // TM

    cond = jnp.zeros((N_COND, D), F32).at[0].set(c_ctx).at[1:1 + n_dec].set(c)
    mod = _ada(cond, ada_w, ada_b)

    def moe(l, x1, h2, final):
        return _moe(x1, h2, router_w[l].T, router_b[l][:, None], mod[l], final_norm_w[None, :], l, exp_gu_w,
                    exp_gu_b, exp_down_w, exp_down_b, n_ctx_tiles, tiles_per_seq, final)

    x1, h2 = _conv_layer(x_prompt.reshape(n_ctx_tok, D), x_sample.reshape(n_dec * dec_len, D), mod[0],
                         norm1_w[0][None, :], conv_in_w[0].astype(BF16), conv_w[0], conv_out_w[0].astype(BF16),
                         norm2_w[0][None, :], n_ctx_tiles, tiles_per_seq)
    x, = moe(0, x1, h2, False)

    w_in = dn_in_w[0]
    w_ba = jnp.zeros((D, LANES), F32).at[:, :4 * HV].set(w_in[:, CONV_CH + V_DIM:]).astype(BF16)
    lane_pad = lambda a: jnp.zeros((1, LANES), F32).at[0, 2 * HV:4 * HV].set(a.reshape(-1))
    q, k, v, z, bg = _dn_in(x, mod[1], norm1_w[1][None, :], w_in[:, :CONV_CH + V_DIM].astype(BF16), w_ba,
                            dn_conv_w[0], lane_pad(dn_a_log[0]), lane_pad(dn_dt_bias[0]), n_ctx_tiles,
                            tiles_per_seq)
    o_f, o_b, s_ctx = _delta(q, k, v, bg, state_delta, n_ctx, ctx_len // CHUNK, n_dec, dec_len // CHUNK)
    x1, h2 = _dn_out(o_f, o_b, z, x, mod[1], dn_norm_w[0][None, :], dn_out_w[0].astype(BF16),
                     norm2_w[1][None, :], n_ctx_tiles, tiles_per_seq)
    y_ctx, y_dec = moe(1, x1, h2, True)
    return y_ctx.reshape(n_ctx, ctx_len, D), y_dec.reshape(n_dec, dec_len, D), s_ctx[:, None]
```
